```python
import jax, jax.numpy as jnp
from jax import lax
import numpy as np

D_MODEL = 1024
BATCH = 8
SEQ = 4096
DEPTH = 1

N_META = 16
BLOCK = 128
META_PAD = BLOCK - N_META
HEAD_DIM = 64
SB_HEADS = (D_MODEL // 2) // HEAD_DIM
RWKV_HEADS = (D_MODEL // 2) // HEAD_DIM
SB_WIDTH = SB_HEADS * HEAD_DIM
RWKV_WIDTH = RWKV_HEADS * HEAD_DIM
MIX_WIDTH = SB_WIDTH + RWKV_WIDTH
D_FF = 2816
W_LORA = 32
A_LORA = 32
G_LORA = 96
N_RWKV_COLS = 3 * RWKV_WIDTH + W_LORA + A_LORA + G_LORA
IN_COLS = 3 * SB_WIDTH + N_RWKV_COLS
RMS_EPS = 1e-6
LNX_EPS = 64e-5

kernel_name = "hymba_sb_rwkv7_macaron"


def rms_norm(x, g):
    xf = x.astype(jnp.float32)
    y = xf * lax.rsqrt(jnp.mean(xf * xf, axis=-1, keepdims=True) + RMS_EPS)
    return (y * g.astype(jnp.float32)).astype(x.dtype)


def swiglu(h, w_gate, w_up, w_down):
    return (jax.nn.silu(h @ w_gate) * (h @ w_up)) @ w_down


def stick_breaking_attention(q, k, v):
    B, L, H, Dh = q.shape
    pad = ((0, 0), (META_PAD, 0), (0, 0), (0, 0))
    q, k, v = [jnp.pad(t, pad).transpose(0, 2, 1, 3) for t in (q, k, v)]
    Lp = L + META_PAD
    nblk = Lp // BLOCK
    key_pos = jnp.arange(Lp)
    scale = Dh ** -0.5
    qb = q.reshape(B, H, nblk, BLOCK, Dh).transpose(2, 0, 1, 3, 4)

    def block(args):
        q_blk, i = args
        q_pos = i * BLOCK + jnp.arange(BLOCK)
        z = jnp.einsum('bhqd,bhkd->bhqk', q_blk, k).astype(jnp.float32) * scale
        valid = (key_pos[None, :] < q_pos[:, None]) & (key_pos[None, :] >= META_PAD)
        log_keep = jnp.where(valid, -jax.nn.softplus(z), 0.0)
        log_rest = lax.cumsum(log_keep, axis=3, reverse=True) - log_keep
        attn = jnp.where(valid, jnp.exp(jax.nn.log_sigmoid(z) + log_rest), 0.0)
        return jnp.einsum('bhqk,bhkd->bhqd', attn.astype(v.dtype), v)

    out = lax.map(block, (qb, jnp.arange(nblk)))
    out = out.transpose(1, 0, 3, 2, 4).reshape(B, Lp, H * Dh)
    return out[:, META_PAD:]


def rwkv7_time_mix(p, mu, w0, w_up, a0, a_up, g_up, k_k, k_a, r_k, lnx_w, lnx_b):
    B, L, _ = p.shape
    C, H, N = RWKV_WIDTH, RWKV_HEADS, HEAD_DIM
    p_prev = jnp.pad(p, ((0, 0), (1, 0), (0, 0)))[:, :-1]
    p = p + (p_prev - p) * mu
    r = p[..., :C]
    k = p[..., C:2 * C]
    v = p[..., 2 * C:3 * C]
    xw = p[..., 3 * C:3 * C + W_LORA]
    xa = p[..., 3 * C + W_LORA:3 * C + W_LORA + A_LORA]
    xg = p[..., 3 * C + W_LORA + A_LORA:]
    w = -jax.nn.softplus(-(w0 + jnp.tanh(xw) @ w_up)) - 0.5
    decay = jnp.exp(-jnp.exp(w.astype(jnp.float32)))
    a = jax.nn.sigmoid(a0 + xa @ a_up)
    g = jax.nn.sigmoid(xg) @ g_up
    kk = (k * k_k).astype(jnp.float32).reshape(B, L, H, N)
    kk = kk / jnp.maximum(jnp.sqrt(jnp.sum(kk * kk, axis=-1, keepdims=True)), 1e-12)
    k = k * (1.0 + (a - 1.0) * k_a)

    rh, kh, vh, ah, dh = [t.astype(jnp.float32).reshape(B, L, H, N) for t in (r, k, v, a, decay)]

    def step(S, inp):
        r_t, w_t, k_t, v_t, kk_t, a_t = inp
        sa = jnp.einsum('bhij,bhj->bhi', S, -kk_t)
        S = (S * w_t[:, :, None, :] + sa[..., None] * (kk_t * a_t)[:, :, None, :]
             + v_t[..., None] * k_t[:, :, None, :])
        return S, jnp.einsum('bhij,bhj->bhi', S, r_t)

    xs = tuple(t.transpose(1, 0, 2, 3) for t in (rh, dh, kh, vh, kk, ah))
    S0 = jnp.zeros((B, H, N, N), jnp.float32)
    _, ys = lax.scan(step, S0, xs)
    y = ys.transpose(1, 0, 2, 3)
    mean = jnp.mean(y, axis=-1, keepdims=True)
    var = jnp.mean(jnp.square(y - mean), axis=-1, keepdims=True)
    y = ((y - mean) * lax.rsqrt(var + LNX_EPS)).reshape(B, L, C)
    y = y * lnx_w.astype(jnp.float32) + lnx_b.astype(jnp.float32)
    bonus = jnp.sum(rh * kh * r_k.astype(jnp.float32), axis=-1, keepdims=True) * vh
    y = y + bonus.reshape(B, L, C)
    return (y * g.astype(jnp.float32)).astype(p.dtype)


def setup_inputs(seed: int = 0) -> dict:
    key = jax.random.key(seed)
    ks = jax.random.split(key, 32)
    nrm = lambda k, s: jax.random.normal(k, s, jnp.float32)
    gain = lambda k, s: 1.0 + 0.02 * nrm(k, s)
    Dd = DEPTH
    return {
        "x": nrm(ks[0], (BATCH, SEQ, D_MODEL)),
        "meta_tokens": nrm(ks[1], (N_META, D_MODEL)),
        "ffn1_norm": gain(ks[2], (Dd, D_MODEL)),
        "ffn1_w_gate": nrm(ks[3], (Dd, D_MODEL, D_FF)) * D_MODEL ** -0.5,
        "ffn1_w_up": nrm(ks[4], (Dd, D_MODEL, D_FF)) * D_MODEL ** -0.5,
        "ffn1_w_down": nrm(ks[5], (Dd, D_FF, D_MODEL)) * D_FF ** -0.5,
        "mix_norm": gain(ks[6], (Dd, D_MODEL)),
        "w_in": nrm(ks[7], (Dd, D_MODEL, IN_COLS)) * D_MODEL ** -0.5,
        "rwkv_mu": jax.random.uniform(ks[8], (Dd, N_RWKV_COLS), jnp.float32),
        "rwkv_w0": jax.random.uniform(ks[9], (Dd, RWKV_WIDTH), jnp.float32, -4.0, 1.0),
        "rwkv_w_up": nrm(ks[10], (Dd, W_LORA, RWKV_WIDTH)) * 0.5 * W_LORA ** -0.5,
        "rwkv_a0": 0.5 * nrm(ks[11], (Dd, RWKV_WIDTH)),
        "rwkv_a_up": nrm(ks[12], (Dd, A_LORA, RWKV_WIDTH)) * 0.5 * A_LORA ** -0.5,
        "rwkv_g_up": nrm(ks[13], (Dd, G_LORA, RWKV_WIDTH)) * G_LORA ** -0.5,
        "rwkv_k_k": 0.85 + 0.05 * nrm(ks[14], (Dd, RWKV_WIDTH)),
        "rwkv_k_a": 1.0 + 0.05 * nrm(ks[15], (Dd, RWKV_WIDTH)),
        "rwkv_r_k": 0.1 * nrm(ks[16], (Dd, RWKV_HEADS, HEAD_DIM)),
        "rwkv_lnx_w": gain(ks[17], (Dd, RWKV_WIDTH)),
        "rwkv_lnx_b": 0.02 * nrm(ks[18], (Dd, RWKV_WIDTH)),
        "w_out": nrm(ks[19], (Dd, MIX_WIDTH, D_MODEL)) * MIX_WIDTH ** -0.5,
        "ffn2_norm": gain(ks[20], (Dd, D_MODEL)),
        "ffn2_w_gate": nrm(ks[21], (Dd, D_MODEL, D_FF)) * D_MODEL ** -0.5,
        "ffn2_w_up": nrm(ks[22], (Dd, D_MODEL, D_FF)) * D_MODEL ** -0.5,
        "ffn2_w_down": nrm(ks[23], (Dd, D_FF, D_MODEL)) * D_FF ** -0.5,
        "final_norm": gain(ks[24], (D_MODEL,)),
    }


def reference(x, meta_tokens, ffn1_norm, ffn1_w_gate, ffn1_w_up, ffn1_w_down, mix_norm, w_in,
              rwkv_mu, rwkv_w0, rwkv_w_up, rwkv_a0, rwkv_a_up, rwkv_g_up, rwkv_k_k, rwkv_k_a,
              rwkv_r_k, rwkv_lnx_w, rwkv_lnx_b, w_out, ffn2_norm, ffn2_w_gate, ffn2_w_up,
              ffn2_w_down, final_norm):
    B = x.shape[0]
    meta = jnp.broadcast_to(meta_tokens[None].astype(x.dtype), (B, N_META, D_MODEL))
    h = jnp.concatenate([meta, x], axis=1)
    for l in range(DEPTH):
        n = rms_norm(h, ffn1_norm[l])
        h = h + 0.5 * swiglu(n, ffn1_w_gate[l], ffn1_w_up[l], ffn1_w_down[l])
        n = rms_norm(h, mix_norm[l])
        proj = n @ w_in[l]
        Bn, L, _ = proj.shape
        q = proj[..., :SB_WIDTH].reshape(Bn, L, SB_HEADS, HEAD_DIM)
        k = proj[..., SB_WIDTH:2 * SB_WIDTH].reshape(Bn, L, SB_HEADS, HEAD_DIM)
        v = proj[..., 2 * SB_WIDTH:3 * SB_WIDTH].reshape(Bn, L, SB_HEADS, HEAD_DIM)
        sb_out = stick_breaking_attention(q, k, v)
        rw_out = rwkv7_time_mix(proj[..., 3 * SB_WIDTH:], rwkv_mu[l], rwkv_w0[l], rwkv_w_up[l],
                                rwkv_a0[l], rwkv_a_up[l], rwkv_g_up[l], rwkv_k_k[l], rwkv_k_a[l],
                                rwkv_r_k[l], rwkv_lnx_w[l], rwkv_lnx_b[l])
        h = h + jnp.concatenate([sb_out, rw_out], axis=-1) @ w_out[l]
        n = rms_norm(h, ffn2_norm[l])
        h = h + 0.5 * swiglu(n, ffn2_w_gate[l], ffn2_w_up[l], ffn2_w_down[l])
    return rms_norm(h, final_norm)[:, N_META:]
```

```python
import functools

import jax
import jax.numpy as jnp
from jax import lax
from jax.experimental import pallas as pl
from jax.experimental.pallas import tpu as pltpu

D_MODEL = 1024
N_META = 16
HEAD_DIM = 64
SB_WIDTH = 512
RWKV_WIDTH = 512
D_FF = 2816
W_LORA = 32
A_LORA = 32
G_LORA = 96
RMS_EPS = 1e-6
LNX_EPS = 64e-5

LANES = 128
META_ROWS = 128
LORA_PAD = 256
CHUNK = 64
PAIR = 2 * HEAD_DIM
N_PAIRS = RWKV_WIDTH // PAIR
VMEM_LIMIT = 56 * 1024 * 1024

F32 = jnp.float32
BF16 = jnp.bfloat16
HI = lax.Precision.HIGHEST


def _const_spec(shape):
    zeros = (0,) * len(shape)
    return pl.BlockSpec(shape, lambda *_: zeros, pipeline_mode=pl.Buffered(1))


def _rms(x, g):
    ms = jnp.mean(x * x, axis=-1, keepdims=True)
    return x * lax.rsqrt(ms + RMS_EPS) * g


def _dot(a, b, precision=None):
    return jnp.dot(a, b, preferred_element_type=F32, precision=precision)


def _dot_nt(a, b, precision=None):
    return lax.dot_general(a, b, (((1,), (1,)), ((), ())), preferred_element_type=F32,
                           precision=precision)


def _dot_tn(a, b, precision=None):
    return lax.dot_general(a, b, (((0,), (0,)), ((), ())), preferred_element_type=F32,
                           precision=precision)


def _swiglu_half_step(h, g, wg_ref, wu_ref, wd_ref):
    n = _rms(h, g).astype(BF16)
    gate = _dot(n, wg_ref[...])
    up = _dot(n, wu_ref[...])
    act = (gate * jax.nn.sigmoid(gate) * up).astype(BF16)
    return h + 0.5 * _dot(act, wd_ref[...])


def _ffn1_kernel(x_ref, g_ref, wg_ref, wu_ref, wd_ref, o_ref):
    o_ref[...] = _swiglu_half_step(x_ref[...], g_ref[...], wg_ref, wu_ref, wd_ref)


def _ffn2_kernel(h_ref, sb_ref, rw_ref, wo_sb_ref, wo_rw_ref, g_ref, wg_ref, wu_ref, wd_ref,
                 gf_ref, o_ref):
    h = h_ref[...] + _dot(sb_ref[...], wo_sb_ref[...]) + _dot(rw_ref[...], wo_rw_ref[...])
    h = _swiglu_half_step(h, g_ref[...], wg_ref, wu_ref, wd_ref)
    o_ref[...] = _rms(h, gf_ref[...])


def _in_proj_kernel(h_ref, g_ref, wqkv_ref, wrkv_ref, wlora_ref, qkv_ref, rkv_ref, lora_ref):
    n = _rms(h_ref[...], g_ref[...]).astype(BF16)
    qkv_ref[...] = _dot(n, wqkv_ref[...]).astype(BF16)
    rkv_ref[...] = _dot(n, wrkv_ref[...])
    lora_ref[...] = _dot(n, wlora_ref[...])


def _token_tile(n_tokens, want):
    t = min(want, n_tokens)
    assert n_tokens % t == 0 and t % 8 == 0, (n_tokens, t)
    return t


def _row_spec(t, width):
    return pl.BlockSpec((t, width), lambda i: (i, 0))


def _ffn1(x2d, g, wg, wu, wd):
    n = x2d.shape[0]
    t = _token_tile(n, 512)
    return pl.pallas_call(
        _ffn1_kernel,
        grid=(n // t,),
        in_specs=[_row_spec(t, D_MODEL), _const_spec((1, D_MODEL)), _const_spec(wg.shape),
                  _const_spec(wu.shape), _const_spec(wd.shape)],
        out_specs=_row_spec(t, D_MODEL),
        out_shape=jax.ShapeDtypeStruct((n, D_MODEL), F32),
        compiler_params=pltpu.CompilerParams(dimension_semantics=("parallel",),
                                             vmem_limit_bytes=VMEM_LIMIT),
        name="ffn1",
    )(x2d, g, wg, wu, wd)


def _ffn2(h2d, sb2d, rw2d, wo_sb, wo_rw, g, wg, wu, wd, gf):
    n = h2d.shape[0]
    t = _token_tile(n, 512)
    return pl.pallas_call(
        _ffn2_kernel,
        grid=(n // t,),
        in_specs=[_row_spec(t, D_MODEL), _row_spec(t, SB_WIDTH), _row_spec(t, RWKV_WIDTH),
                  _const_spec(wo_sb.shape), _const_spec(wo_rw.shape), _const_spec((1, D_MODEL)),
                  _const_spec(wg.shape), _const_spec(wu.shape), _const_spec(wd.shape),
                  _const_spec((1, D_MODEL))],
        out_specs=_row_spec(t, D_MODEL),
        out_shape=jax.ShapeDtypeStruct((n, D_MODEL), F32),
        compiler_params=pltpu.CompilerParams(dimension_semantics=("parallel",),
                                             vmem_limit_bytes=VMEM_LIMIT),
        name="ffn2",
    )(h2d, sb2d, rw2d, wo_sb, wo_rw, g, wg, wu, wd, gf)


def _in_proj(h2d, g, wqkv, wrkv, wlora):
    n = h2d.shape[0]
    t = _token_tile(n, 512)
    return pl.pallas_call(
        _in_proj_kernel,
        grid=(n // t,),
        in_specs=[_row_spec(t, D_MODEL), _const_spec((1, D_MODEL)), _const_spec(wqkv.shape),
                  _const_spec(wrkv.shape), _const_spec(wlora.shape)],
        out_specs=[_row_spec(t, 3 * SB_WIDTH), _row_spec(t, 3 * RWKV_WIDTH),
                   _row_spec(t, LORA_PAD)],
        out_shape=[jax.ShapeDtypeStruct((n, 3 * SB_WIDTH), BF16),
                   jax.ShapeDtypeStruct((n, 3 * RWKV_WIDTH), F32),
                   jax.ShapeDtypeStruct((n, LORA_PAD), F32)],
        compiler_params=pltpu.CompilerParams(dimension_semantics=("parallel",),
                                             vmem_limit_bytes=VMEM_LIMIT),
        name="in_proj",
    )(h2d, g, wqkv, wrkv, wlora)


def _sb_attn_kernel(q_ref, k_ref, v_ref, km_ref, vm_ref, o_ref, *, tile):
    i = pl.program_id(2)
    q = q_ref[0]
    low = lax.broadcasted_iota(jnp.int32, (1, LANES), 1) < HEAD_DIM
    zero = jnp.zeros_like(q)
    q_heads = (jnp.where(low, q, zero), jnp.where(low, zero, q))

    row = lax.broadcasted_iota(jnp.int32, (tile, tile), 0)
    col = lax.broadcasted_iota(jnp.int32, (tile, tile), 1)
    suffix_w = jnp.concatenate(
        [jnp.where(row > col, -1.0, 0.0), jnp.full((tile, tile), -1.0, F32)], axis=1).astype(BF16)
    diag_mask = col < row
    meta_mask = col >= META_ROWS - N_META

    def sweep(kb, vb, mask, carry):
        out = []
        for hh in range(2):
            acc, rest = carry[hh]
            z = _dot_nt(q_heads[hh], kb)
            sp = jnp.maximum(z, 0.0) + jnp.log(1.0 + jnp.exp(-jnp.abs(z)))
            if mask is not None:
                sp = jnp.where(mask, sp, 0.0)
            sp_hi = sp.astype(BF16)
            sp_lo = (sp - sp_hi.astype(F32)).astype(BF16)
            cw = _dot(sp_hi, suffix_w) + _dot(sp_lo, suffix_w)
            p = jnp.exp(z - sp + cw[:, :tile] + rest)
            if mask is not None:
                p = jnp.where(mask, p, 0.0)
            out.append((acc + _dot(p.astype(BF16), vb), rest + cw[:, tile:]))
        return tuple(out)

    zeros = jnp.zeros((tile, LANES), F32)
    carry = ((zeros, jnp.zeros((tile, tile), F32)),) * 2
    start = pl.multiple_of(i * tile, tile)
    carry = sweep(k_ref[0, pl.ds(start, tile), :], v_ref[0, pl.ds(start, tile), :], diag_mask,
                  carry)

    def body(t, carry):
        start = pl.multiple_of((i - 1 - t) * tile, tile)
        return sweep(k_ref[0, pl.ds(start, tile), :], v_ref[0, pl.ds(start, tile), :], None, carry)

    carry = lax.fori_loop(0, i, body, carry)
    carry = sweep(km_ref[...], vm_ref[...], meta_mask, carry)
    o_ref[0] = jnp.where(low, carry[0][0], carry[1][0]).astype(o_ref.dtype)


def _sb_attention(qkv, qkv_meta):
    b, l, _ = qkv.shape
    tile = META_ROWS
    assert l % tile == 0
    n_pairs = SB_WIDTH // LANES
    return pl.pallas_call(
        functools.partial(_sb_attn_kernel, tile=tile),
        grid=(b, n_pairs, l // tile),
        in_specs=[
            pl.BlockSpec((1, tile, LANES), lambda bi, hp, i: (bi, i, hp)),
            pl.BlockSpec((1, l, LANES), lambda bi, hp, i: (bi, 0, n_pairs + hp)),
            pl.BlockSpec((1, l, LANES), lambda bi, hp, i: (bi, 0, 2 * n_pairs + hp)),
            pl.BlockSpec((tile, LANES), lambda bi, hp, i: (0, n_pairs + hp)),
            pl.BlockSpec((tile, LANES), lambda bi, hp, i: (0, 2 * n_pairs + hp)),
        ],
        out_specs=pl.BlockSpec((1, tile, LANES), lambda bi, hp, i: (bi, i, hp)),
        out_shape=jax.ShapeDtypeStruct((b, l, SB_WIDTH), BF16),
        compiler_params=pltpu.CompilerParams(
            dimension_semantics=("parallel", "parallel", "arbitrary"),
            vmem_limit_bytes=VMEM_LIMIT),
        name="sb_attention",
    )(qkv, qkv, qkv, qkv_meta, qkv_meta)


def _unit_lower_inverse(n_mat, eye):
    inv = eye + n_mat
    power = n_mat
    for _ in range(5):
        power = _dot(power, power, HI)
        inv = inv + _dot(inv, power, HI)
    return inv


def _rwkv_kernel(rkv_ref, lora_ref, prev_rkv_ref, prev_lora_ref, h0_ref, mu_rkv_ref, mu_lora_ref,
                 w0_ref, wup_ref, a0_ref, aup_ref, gup_ref, kk_ref, ka_ref, rk_ref, lnw_ref,
                 lnb_ref, y_ref, hT_ref, h_sc, prev_rkv_sc, prev_lora_sc, r_sc, k_sc, v_sc,
                 kk_sc, a_sc, lw_sc, y_sc, *, tile):
    c_w = RWKV_WIDTH

    @pl.when(pl.program_id(1) == 0)
    def _():
        h_sc[...] = h0_ref[...]
        prev_rkv_sc[...] = prev_rkv_ref[...]
        prev_lora_sc[...] = prev_lora_ref[...]

    first_row = lax.broadcasted_iota(jnp.int32, (tile, 1), 0) == 0

    def token_shift(p, prev_sc, mu):
        p_prev = jnp.where(first_row, prev_sc[...], pltpu.roll(p, 1, axis=0))
        prev_sc[...] = p[tile - 1:tile, :]
        return p + (p_prev - p) * mu

    p = token_shift(rkv_ref[0], prev_rkv_sc, mu_rkv_ref[...])
    lora = token_shift(lora_ref[0], prev_lora_sc, mu_lora_ref[...])
    r = p[:, :c_w]
    k = p[:, c_w:2 * c_w]
    v = p[:, 2 * c_w:]
    w = -jax.nn.softplus(-(w0_ref[...] + _dot(jnp.tanh(lora), wup_ref[...], HI))) - 0.5
    a = jax.nn.sigmoid(a0_ref[...] + _dot(lora, aup_ref[...], HI))
    gate = _dot(jax.nn.sigmoid(lora), gup_ref[...], HI)

    hr = lax.broadcasted_iota(jnp.int32, (c_w, c_w), 0) // HEAD_DIM
    hc = lax.broadcasted_iota(jnp.int32, (c_w, c_w), 1) // HEAD_DIM
    head_sum = jnp.where(hr == hc, 1.0, 0.0)

    kk = k * kk_ref[...]
    kk = kk * lax.rsqrt(jnp.maximum(_dot(kk * kk, head_sum, HI), 1e-24))
    k = k * (1.0 + (a - 1.0) * ka_ref[...])
    r_sc[...] = r
    k_sc[...] = k
    v_sc[...] = v
    kk_sc[...] = kk
    a_sc[...] = a
    lw_sc[...] = -jnp.exp(w)

    rr = lax.broadcasted_iota(jnp.int32, (PAIR, PAIR), 0)
    cc = lax.broadcasted_iota(jnp.int32, (PAIR, PAIR), 1)
    same_head = (rr < HEAD_DIM) == (cc < HEAD_DIM)
    strict_lower = same_head & (cc < rr)
    lower = same_head & (cc <= rr)
    eye_mask = rr == cc
    eye = jnp.where(eye_mask, 1.0, 0.0)
    tr = lax.broadcasted_iota(jnp.int32, (CHUNK, CHUNK), 0)
    tc = lax.broadcasted_iota(jnp.int32, (CHUNK, CHUNK), 1)
    cumsum_w = jnp.where(tc <= tr, 1.0, 0.0)

    def chunk(c, _):
        rows = pl.ds(pl.multiple_of(c * CHUNK, CHUNK), CHUNK)
        lw = lw_sc[rows, :]
        g = _dot(cumsum_w, lw, HI)
        g_end = g[CHUNK - 1:CHUNK, :]
        decay_in = jnp.exp(g)
        decay_out = jnp.exp(-g)
        decay_end = jnp.exp(g_end - g)
        kk_c = kk_sc[rows, :]
        kka = kk_c * a_sc[rows, :]
        k_c = k_sc[rows, :]
        a_bar = -kk_c * jnp.exp(g - lw)
        r_bar = r_sc[rows, :] * decay_in
        b_til = kka * decay_out
        k_til = k_c * decay_out
        b_hat = kka * decay_end
        k_hat = k_c * decay_end
        v_c = v_sc[rows, :]
        gamma_end = jnp.exp(g_end)

        for pr in range(N_PAIRS):
            sl = slice(PAIR * pr, PAIR * (pr + 1))

            def stack(x, masked):
                x2 = jnp.concatenate([x[:, sl], x[:, sl]], axis=0)
                return jnp.where(same_head, x2, 0.0) if masked else x2

            a2 = stack(a_bar, True)
            r2 = stack(r_bar, True)
            v2 = stack(v_c, True)
            s1 = _dot_nt(jnp.concatenate([a2, r2], axis=0),
                         jnp.concatenate([stack(b_til, False), stack(k_til, False)], axis=0), HI)
            n_mat = jnp.where(strict_lower, s1[:PAIR, :PAIR], 0.0)
            a_ak = jnp.where(strict_lower, s1[:PAIR, PAIR:], 0.0)
            a_rb = jnp.where(lower, s1[PAIR:, :PAIR], 0.0)
            a_rk = jnp.where(lower, s1[PAIR:, PAIR:], 0.0)
            inv = _unit_lower_inverse(n_mat, eye)
            pq = _dot(inv, jnp.concatenate([a2, _dot(a_ak, v2, HI)], axis=1), HI)
            ef = _dot(a_rb, pq, HI) + jnp.concatenate([r2, _dot(a_rk, v2, HI)], axis=1)
            ef = ef[:CHUNK] + ef[CHUNK:]
            mg = _dot_tn(stack(b_hat, True), pq, HI)
            m_mat = mg[:, :PAIR] + jnp.where(eye_mask, gamma_end[:, sl], 0.0)
            g_mat = mg[:, PAIR:] + _dot_tn(stack(k_hat, True), v2, HI)
            eh = _dot(jnp.concatenate([ef[:, :PAIR], m_mat], axis=0), h_sc[pr], HI)
            y_sc[rows, sl] = eh[:CHUNK] + ef[:, PAIR:]
            h_sc[pr] = eh[CHUNK:] + g_mat
        return 0

    lax.fori_loop(0, tile // CHUNK, chunk, 0)

    y = y_sc[...]
    head_mean = head_sum * (1.0 / HEAD_DIM)
    d = y - _dot(y, head_mean, HI)
    yn = d * lax.rsqrt(_dot(d * d, head_mean, HI) + LNX_EPS)
    yn = yn * lnw_ref[...] + lnb_ref[...]
    bonus = _dot(r_sc[...] * k_sc[...] * rk_ref[...], head_sum, HI) * v_sc[...]
    y_ref[0] = ((yn + bonus) * gate).astype(y_ref.dtype)
    hT_ref[0] = h_sc[...]


def _rwkv(rkv, lora, prev_rkv, prev_lora, h0, params):
    b, l, _ = rkv.shape
    tile = min(512, l)
    assert l % tile == 0 and tile % CHUNK == 0
    seq = lambda w: pl.BlockSpec((1, tile, w), lambda bi, t: (bi, t, 0))
    scr = lambda: pltpu.VMEM((tile, RWKV_WIDTH), F32)
    return pl.pallas_call(
        functools.partial(_rwkv_kernel, tile=tile),
        grid=(b, l // tile),
        in_specs=[seq(3 * RWKV_WIDTH), seq(LORA_PAD), _const_spec(prev_rkv.shape),
                  _const_spec(prev_lora.shape), _const_spec(h0.shape)]
                 + [_const_spec(p.shape) for p in params],
        out_specs=[seq(RWKV_WIDTH),
                   pl.BlockSpec((1, N_PAIRS, PAIR, PAIR), lambda bi, t: (bi, 0, 0, 0))],
        out_shape=[jax.ShapeDtypeStruct((b, l, RWKV_WIDTH), BF16),
                   jax.ShapeDtypeStruct((b, N_PAIRS, PAIR, PAIR), F32)],
        scratch_shapes=[pltpu.VMEM((N_PAIRS, PAIR, PAIR), F32),
                        pltpu.VMEM((1, 3 * RWKV_WIDTH), F32), pltpu.VMEM((1, LORA_PAD), F32),
                        scr(), scr(), scr(), scr(), scr(), scr(), scr()],
        compiler_params=pltpu.CompilerParams(dimension_semantics=("parallel", "arbitrary"),
                                             vmem_limit_bytes=VMEM_LIMIT),
        name="rwkv7",
    )(rkv, lora, prev_rkv, prev_lora, h0, *params)


def _pad_rows(w, start, total):
    return jnp.pad(w, ((start, total - start - w.shape[0]), (0, 0)))


def kernel(x, meta_tokens, ffn1_norm, ffn1_w_gate, ffn1_w_up, ffn1_w_down, mix_norm, w_in, rwkv_mu, rwkv_w0, rwkv_w_up, rwkv_a0, rwkv_a_up, rwkv_g_up, rwkv_k_k, rwkv_k_a, rwkv_r_k, rwkv_lnx_w, rwkv_lnx_b, w_out, ffn2_norm, ffn2_w_gate, ffn2_w_up, ffn2_w_down, final_norm):
    b, l, _ = x.shape
    depth = ffn1_norm.shape[0]
    assert depth == 1, "stacked layers would need the meta rows carried through the mixers"
    row =lambda p: p.reshape(1, -1).astype(F32)
    n_rkv = 3 * RWKV_WIDTH
    n_lora = W_LORA + A_LORA + G_LORA

    h = x.reshape(b * l, D_MODEL)
    h_meta = _pad_rows(meta_tokens.astype(F32), META_ROWS - N_META, META_ROWS)
    for d in range(depth):
        w_qkv = w_in[d][:, :3 * SB_WIDTH]
        w_qkv = w_qkv.at[:, :SB_WIDTH].multiply(HEAD_DIM ** -0.5).astype(BF16)
        w_rkv = w_in[d][:, 3 * SB_WIDTH:3 * SB_WIDTH + n_rkv].astype(BF16)
        w_lora = jnp.pad(w_in[d][:, 3 * SB_WIDTH + n_rkv:],
                         ((0, 0), (0, LORA_PAD - n_lora))).astype(BF16)
        mu_rkv = row(rwkv_mu[d][:n_rkv])
        mu_lora = jnp.pad(row(rwkv_mu[d][n_rkv:]), ((0, 0), (0, LORA_PAD - n_lora)))
        rwkv_params = (
            mu_rkv, mu_lora, row(rwkv_w0[d]), _pad_rows(rwkv_w_up[d], 0, LORA_PAD),
            row(rwkv_a0[d]), _pad_rows(rwkv_a_up[d], W_LORA, LORA_PAD),
            _pad_rows(rwkv_g_up[d], W_LORA + A_LORA, LORA_PAD), row(rwkv_k_k[d]),
            row(rwkv_k_a[d]), row(rwkv_r_k[d]), row(rwkv_lnx_w[d]), row(rwkv_lnx_b[d]))
        ffn1_w = (row(ffn1_norm[d]), ffn1_w_gate[d].astype(BF16), ffn1_w_up[d].astype(BF16),
                  ffn1_w_down[d].astype(BF16))
        ffn2_w = (row(ffn2_norm[d]), ffn2_w_gate[d].astype(BF16), ffn2_w_up[d].astype(BF16),
                  ffn2_w_down[d].astype(BF16))
        wo_sb = w_out[d][:SB_WIDTH].astype(BF16)
        wo_rw = w_out[d][SB_WIDTH:].astype(BF16)
        gf = row(final_norm)

        h = _ffn1(h, *ffn1_w)
        h_meta = _ffn1(h_meta, *ffn1_w)
        qkv, rkv, lora = _in_proj(h, row(mix_norm[d]), w_qkv, w_rkv, w_lora)
        qkv_m, rkv_m, lora_m = _in_proj(h_meta, row(mix_norm[d]), w_qkv, w_rkv, w_lora)

        sb = _sb_attention(qkv.reshape(b, l, -1), qkv_m)

        zero_state = jnp.zeros((N_PAIRS, PAIR, PAIR), F32)
        _, h_meta_state = _rwkv(rkv_m[None], lora_m[None], jnp.zeros((1, n_rkv), F32),
                                jnp.zeros((1, LORA_PAD), F32), zero_state, rwkv_params)
        rw, _ = _rwkv(rkv.reshape(b, l, -1), lora.reshape(b, l, -1), rkv_m[-1:], lora_m[-1:],
                      h_meta_state[0], rwkv_params)

        h = _ffn2(h, sb.reshape(b * l, -1), rw.reshape(b * l, -1), wo_sb, wo_rw, *ffn2_w, gf)
    return h.reshape(b, l, D_MODEL)
```

```python
import functools

import jax
import jax.numpy as jnp
from jax import lax
from jax.experimental import pallas as pl
from jax.experimental.pallas import tpu as pltpu

D_MODEL = 1024
N_META = 16
HEAD_DIM = 64
SB_WIDTH = 512
RWKV_WIDTH = 512
D_FF = 2816
W_LORA = 32
A_LORA = 32
G_LORA = 96
RMS_EPS = 1e-6
LNX_EPS = 64e-5

LANES = 128
META_ROWS = 128
LORA_PAD = 256
CHUNK = 64
PAIR = 2 * HEAD_DIM
N_PAIRS = RWKV_WIDTH // PAIR
VMEM_LIMIT = 56 * 1024 * 1024
SB_Q_TILE = 1024
SB_K_TILE = 256

F32 = jnp.float32
BF16 = jnp.bfloat16
HI = lax.Precision.HIGHEST
LOG2_E = 1.4426950408889634


def _const_spec(shape):
    zeros = (0,) * len(shape)
    return pl.BlockSpec(shape, lambda *_: zeros, pipeline_mode=pl.Buffered(1))


def _rms(x, g):
    ms = jnp.mean(x * x, axis=-1, keepdims=True)
    return x * lax.rsqrt(ms + RMS_EPS) * g


def _dot(a, b, precision=None):
    return jnp.dot(a, b, preferred_element_type=F32, precision=precision)


def _dot_nt(a, b, precision=None):
    return lax.dot_general(a, b, (((1,), (1,)), ((), ())), preferred_element_type=F32,
                           precision=precision)


def _dot_tn(a, b, precision=None):
    return lax.dot_general(a, b, (((0,), (0,)), ((), ())), preferred_element_type=F32,
                           precision=precision)


def _swiglu_half_step(h, g, wg_ref, wu_ref, wd_ref):
    n = _rms(h, g).astype(BF16)
    gate = _dot(n, wg_ref[...])
    up = _dot(n, wu_ref[...])
    act = (gate * jax.nn.sigmoid(gate) * up).astype(BF16)
    return h + 0.5 * _dot(act, wd_ref[...])


def _ffn1_kernel(x_ref, g_ref, wg_ref, wu_ref, wd_ref, o_ref):
    o_ref[...] = _swiglu_half_step(x_ref[...], g_ref[...], wg_ref, wu_ref, wd_ref)


def _ffn2_kernel(h_ref, sb_ref, rw_ref, wo_sb_ref, wo_rw_ref, g_ref, wg_ref, wu_ref, wd_ref,
                 gf_ref, o_ref):
    h = h_ref[...] + _dot(sb_ref[...], wo_sb_ref[...]) + _dot(rw_ref[...], wo_rw_ref[...])
    h = _swiglu_half_step(h, g_ref[...], wg_ref, wu_ref, wd_ref)
    o_ref[...] = _rms(h, gf_ref[...])


def _in_proj_kernel(h_ref, g_ref, wqkv_ref, wrkv_ref, wlora_ref, qkv_ref, rkv_ref, lora_ref):
    n = _rms(h_ref[...], g_ref[...]).astype(BF16)
    qkv_ref[...] = _dot(n, wqkv_ref[...]).astype(BF16)
    rkv_ref[...] = _dot(n, wrkv_ref[...])
    lora_ref[...] = _dot(n, wlora_ref[...])


def _token_tile(n_tokens, want):
    t = min(want, n_tokens)
    assert n_tokens % t == 0 and t % 8 == 0, (n_tokens, t)
    return t


def _row_spec(t, width):
    return pl.BlockSpec((t, width), lambda i: (i, 0))


def _ffn1(x2d, g, wg, wu, wd):
    n = x2d.shape[0]
    t = _token_tile(n, 512)
    return pl.pallas_call(
        _ffn1_kernel,
        grid=(n // t,),
        in_specs=[_row_spec(t, D_MODEL), _const_spec((1, D_MODEL)), _const_spec(wg.shape),
                  _const_spec(wu.shape), _const_spec(wd.shape)],
        out_specs=_row_spec(t, D_MODEL),
        out_shape=jax.ShapeDtypeStruct((n, D_MODEL), F32),
        compiler_params=pltpu.CompilerParams(dimension_semantics=("parallel",),
                                             vmem_limit_bytes=VMEM_LIMIT),
        name="ffn1",
    )(x2d, g, wg, wu, wd)


def _ffn2(h2d, sb2d, rw2d, wo_sb, wo_rw, g, wg, wu, wd, gf):
    n = h2d.shape[0]
    t = _token_tile(n, 512)
    return pl.pallas_call(
        _ffn2_kernel,
        grid=(n // t,),
        in_specs=[_row_spec(t, D_MODEL), _row_spec(t, SB_WIDTH), _row_spec(t, RWKV_WIDTH),
                  _const_spec(wo_sb.shape), _const_spec(wo_rw.shape), _const_spec((1, D_MODEL)),
                  _const_spec(wg.shape), _const_spec(wu.shape), _const_spec(wd.shape),
                  _const_spec((1, D_MODEL))],
        out_specs=_row_spec(t, D_MODEL),
        out_shape=jax.ShapeDtypeStruct((n, D_MODEL), F32),
        compiler_params=pltpu.CompilerParams(dimension_semantics=("parallel",),
                                             vmem_limit_bytes=VMEM_LIMIT),
        name="ffn2",
    )(h2d, sb2d, rw2d, wo_sb, wo_rw, g, wg, wu, wd, gf)


def _in_proj(h2d, g, wqkv, wrkv, wlora):
    n = h2d.shape[0]
    t = _token_tile(n, 512)
    return pl.pallas_call(
        _in_proj_kernel,
        grid=(n // t,),
        in_specs=[_row_spec(t, D_MODEL), _const_spec((1, D_MODEL)), _const_spec(wqkv.shape),
                  _const_spec(wrkv.shape), _const_spec(wlora.shape)],
        out_specs=[_row_spec(t, 3 * SB_WIDTH), _row_spec(t, 3 * RWKV_WIDTH),
                   _row_spec(t, LORA_PAD)],
        out_shape=[jax.ShapeDtypeStruct((n, 3 * SB_WIDTH), BF16),
                   jax.ShapeDtypeStruct((n, 3 * RWKV_WIDTH), F32),
                   jax.ShapeDtypeStruct((n, LORA_PAD), F32)],
        compiler_params=pltpu.CompilerParams(dimension_semantics=("parallel",),
                                             vmem_limit_bytes=VMEM_LIMIT),
        name="in_proj",
    )(h2d, g, wqkv, wrkv, wlora)


def _sb_attn_kernel(q_ref, k_ref, v_ref, km_ref, vm_ref, o_ref, qh_sc, acc_sc, rest_sc, *, tq, tk):
    i = pl.program_id(2)
    q = q_ref[0]
    low = lax.broadcasted_iota(jnp.int32, (1, LANES), 1) < HEAD_DIM
    zero = jnp.zeros_like(q)
    qh_sc[0] = jnp.where(low, q, zero)
    qh_sc[1] = jnp.where(low, zero, q)
    acc_sc[...] = jnp.zeros_like(acc_sc)
    rest_sc[...] = jnp.zeros_like(rest_sc)

    row = lax.broadcasted_iota(jnp.int32, (tk, tk), 0)
    col = lax.broadcasted_iota(jnp.int32, (tk, tk), 1)
    suffix_w = jnp.where(row > col, -1.0, 0.0).astype(BF16)
    diag_mask = col < row
    meta_mask = lax.broadcasted_iota(jnp.int32, (1, META_ROWS), 1) >= META_ROWS - N_META

    def sweep(r0, nrows, kb, vb, w, mask):
        nk = kb.shape[0]
        rows = slice(r0, r0 + nrows)
        for hh in range(2):
            z = _dot_nt(qh_sc[hh, rows, :], kb)
            neg_abs = pltpu.bitcast(pltpu.bitcast(z, jnp.uint32) | jnp.uint32(0x80000000), F32)
            sp = jnp.maximum(z, 0.0) + jnp.log2(1.0 + jnp.exp2(neg_abs))
            if mask is not None:
                sp = jnp.where(mask, sp, 0.0)
            cum = _dot(sp.astype(BF16), w)
            p = jnp.exp2(z - sp + cum)
            if mask is not None:
                p = jnp.where(mask, p, 0.0)
            rest = rest_sc[hh, rows, :]
            acc_sc[hh, rows, :] += jnp.exp2(rest) * _dot(p.astype(BF16), vb)
            rest_sc[hh, rows, :] = rest + (cum[:, :1] - sp[:, :1])

    def key_tile(j):
        start = pl.multiple_of(j * tk, tk)
        return k_ref[0, pl.ds(start, tk), :], v_ref[0, pl.ds(start, tk), :]

    n_diag = tq // tk
    for jj in reversed(range(n_diag)):
        kb, vb = key_tile(i * n_diag + jj)
        sweep(jj * tk, tk, kb, vb, suffix_w, diag_mask)
        if jj + 1 < n_diag:
            sweep((jj + 1) * tk, tq - (jj + 1) * tk, kb, vb, suffix_w, None)

    def body(t, _):
        for jj in reversed(range(n_diag)):
            kb, vb = key_tile((i - 1 - t) * n_diag + jj)
            sweep(0, tq, kb, vb, suffix_w, None)
        return 0

    lax.fori_loop(0, i, body, 0)
    sweep(0, tq, km_ref[...], vm_ref[...], suffix_w[tk - META_ROWS:, tk - META_ROWS:], meta_mask)
    o_ref[0] = jnp.where(low, acc_sc[0], acc_sc[1]).astype(o_ref.dtype)


def _sb_attention(qkv, qkv_meta):
    b, l, _ = qkv.shape
    tq = min(SB_Q_TILE, l)
    tk = min(SB_K_TILE, l)
    assert l % tq == 0 and tq % tk == 0 and tk >= META_ROWS
    n_pairs = SB_WIDTH // LANES
    return pl.pallas_call(
        functools.partial(_sb_attn_kernel, tq=tq, tk=tk),
        grid=(b, n_pairs, l // tq),
        in_specs=[
            pl.BlockSpec((1, tq, LANES), lambda bi, hp, i: (bi, i, hp)),
            pl.BlockSpec((1, l, LANES), lambda bi, hp, i: (bi, 0, n_pairs + hp)),
            pl.BlockSpec((1, l, LANES), lambda bi, hp, i: (bi, 0, 2 * n_pairs + hp)),
            pl.BlockSpec((META_ROWS, LANES), lambda bi, hp, i: (0, n_pairs + hp)),
            pl.BlockSpec((META_ROWS, LANES), lambda bi, hp, i: (0, 2 * n_pairs + hp)),
        ],
        out_specs=pl.BlockSpec((1, tq, LANES), lambda bi, hp, i: (bi, i, hp)),
        out_shape=jax.ShapeDtypeStruct((b, l, SB_WIDTH), BF16),
        scratch_shapes=[pltpu.VMEM((2, tq, LANES), BF16), pltpu.VMEM((2, tq, LANES), F32),
                        pltpu.VMEM((2, tq, 1), F32)],
        compiler_params=pltpu.CompilerParams(
            dimension_semantics=("parallel", "parallel", "arbitrary"),
            vmem_limit_bytes=VMEM_LIMIT),
        name="sb_attention",
    )(qkv, qkv, qkv, qkv_meta, qkv_meta)


def _unit_lower_inverse(n_mat, eye):
    inv = eye + n_mat
    power = n_mat
    for _ in range(5):
        power = _dot(power, power, HI)
        inv = inv + _dot(inv, power, HI)
    return inv


def _rwkv_kernel(rkv_ref, lora_ref, prev_rkv_ref, prev_lora_ref, h0_ref, mu_rkv_ref, mu_lora_ref,
                 w0_ref, wup_ref, a0_ref, aup_ref, gup_ref, kk_ref, ka_ref, rk_ref, lnw_ref,
                 lnb_ref, y_ref, hT_ref, h_sc, prev_rkv_sc, prev_lora_sc, r_sc, k_sc, v_sc,
                 kk_sc, a_sc, lw_sc, y_sc, *, tile):
    c_w = RWKV_WIDTH

    @pl.when(pl.program_id(1) == 0)
    def _():
        h_sc[...] = h0_ref[...]
        prev_rkv_sc[...] = prev_rkv_ref[...]
        prev_lora_sc[...] = prev_lora_ref[...]

    first_row = lax.broadcasted_iota(jnp.int32, (tile, 1), 0) == 0

    def token_shift(p, prev_sc, mu):
        p_prev = jnp.where(first_row, prev_sc[...], pltpu.roll(p, 1, axis=0))
        prev_sc[...] = p[tile - 1:tile, :]
        return p + (p_prev - p) * mu

    p = token_shift(rkv_ref[0], prev_rkv_sc, mu_rkv_ref[...])
    lora = token_shift(lora_ref[0], prev_lora_sc, mu_lora_ref[...])
    r = p[:, :c_w]
    k = p[:, c_w:2 * c_w]
    v = p[:, 2 * c_w:]
    w = -jax.nn.softplus(-(w0_ref[...] + _dot(jnp.tanh(lora), wup_ref[...], HI))) - 0.5
    a = jax.nn.sigmoid(a0_ref[...] + _dot(lora, aup_ref[...], HI))
    gate = _dot(jax.nn.sigmoid(lora), gup_ref[...], HI)

    hr = lax.broadcasted_iota(jnp.int32, (c_w, c_w), 0) // HEAD_DIM
    hc = lax.broadcasted_iota(jnp.int32, (c_w, c_w), 1) // HEAD_DIM
    head_sum = jnp.where(hr == hc, 1.0, 0.0)

    kk = k * kk_ref[...]
    kk = kk * lax.rsqrt(jnp.maximum(_dot(kk * kk, head_sum, HI), 1e-24))
    k = k * (1.0 + (a - 1.0) * ka_ref[...])
    r_sc[...] = r
    k_sc[...] = k
    v_sc[...] = v
    kk_sc[...] = kk
    a_sc[...] = a
    lw_sc[...] = -jnp.exp(w)

    rr = lax.broadcasted_iota(jnp.int32, (PAIR, PAIR), 0)
    cc = lax.broadcasted_iota(jnp.int32, (PAIR, PAIR), 1)
    same_head = (rr < HEAD_DIM) == (cc < HEAD_DIM)
    strict_lower = same_head & (cc < rr)
    lower = same_head & (cc <= rr)
    eye_mask = rr == cc
    eye = jnp.where(eye_mask, 1.0, 0.0)
    tr = lax.broadcasted_iota(jnp.int32, (CHUNK, CHUNK), 0)
    tc = lax.broadcasted_iota(jnp.int32, (CHUNK, CHUNK), 1)
    cumsum_w = jnp.where(tc <= tr, 1.0, 0.0)

    def chunk(c, _):
        rows = pl.ds(pl.multiple_of(c * CHUNK, CHUNK), CHUNK)
        lw = lw_sc[rows, :]
        g = _dot(cumsum_w, lw, HI)
        g_end = g[CHUNK - 1:CHUNK, :]
        decay_in = jnp.exp(g)
        decay_out = jnp.exp(-g)
        decay_end = jnp.exp(g_end - g)
        kk_c = kk_sc[rows, :]
        kka = kk_c * a_sc[rows, :]
        k_c = k_sc[rows, :]
        a_bar = -kk_c * jnp.exp(g - lw)
        r_bar = r_sc[rows, :] * decay_in
        b_til = kka * decay_out
        k_til = k_c * decay_out
        b_hat = kka * decay_end
        k_hat = k_c * decay_end
        v_c = v_sc[rows, :]
        gamma_end = jnp.exp(g_end)

        for pr in range(N_PAIRS):
            sl = slice(PAIR * pr, PAIR * (pr + 1))

            def stack(x, masked):
                x2 = jnp.concatenate([x[:, sl], x[:, sl]], axis=0)
                return jnp.where(same_head, x2, 0.0) if masked else x2

            a2 = stack(a_bar, True)
            r2 = stack(r_bar, True)
            v2 = stack(v_c, True)
            s1 = _dot_nt(jnp.concatenate([a2, r2], axis=0),
                         jnp.concatenate([stack(b_til, False), stack(k_til, False)], axis=0), HI)
            n_mat = jnp.where(strict_lower, s1[:PAIR, :PAIR], 0.0)
            a_ak = jnp.where(strict_lower, s1[:PAIR, PAIR:], 0.0)
            a_rb = jnp.where(lower, s1[PAIR:, :PAIR], 0.0)
            a_rk = jnp.where(lower, s1[PAIR:, PAIR:], 0.0)
            inv = _unit_lower_inverse(n_mat, eye)
            pq = _dot(inv, jnp.concatenate([a2, _dot(a_ak, v2, HI)], axis=1), HI)
            ef = _dot(a_rb, pq, HI) + jnp.concatenate([r2, _dot(a_rk, v2, HI)], axis=1)
            ef = ef[:CHUNK] + ef[CHUNK:]
            mg = _dot_tn(stack(b_hat, True), pq, HI)
            m_mat = mg[:, :PAIR] + jnp.where(eye_mask, gamma_end[:, sl], 0.0)
            g_mat = mg[:, PAIR:] + _dot_tn(stack(k_hat, True), v2, HI)
            eh = _dot(jnp.concatenate([ef[:, :PAIR], m_mat], axis=0), h_sc[pr], HI)
            y_sc[rows, sl] = eh[:CHUNK] + ef[:, PAIR:]
            h_sc[pr] = eh[CHUNK:] + g_mat
        return 0

    lax.fori_loop(0, tile // CHUNK, chunk, 0)

    y = y_sc[...]
    head_mean = head_sum * (1.0 / HEAD_DIM)
    d = y - _dot(y, head_mean, HI)
    yn = d * lax.rsqrt(_dot(d * d, head_mean, HI) + LNX_EPS)
    yn = yn * lnw_ref[...] + lnb_ref[...]
    bonus = _dot(r_sc[...] * k_sc[...] * rk_ref[...], head_sum, HI) * v_sc[...]
    y_ref[0] = ((yn + bonus) * gate).astype(y_ref.dtype)
    hT_ref[0] = h_sc[...]


def _rwkv(rkv, lora, prev_rkv, prev_lora, h0, params):
    b, l, _ = rkv.shape
    tile = min(512, l)
    assert l % tile == 0 and tile % CHUNK == 0
    seq = lambda w: pl.BlockSpec((1, tile, w), lambda bi, t: (bi, t, 0))
    scr = lambda: pltpu.VMEM((tile, RWKV_WIDTH), F32)
    return pl.pallas_call(
        functools.partial(_rwkv_kernel, tile=tile),
        grid=(b, l // tile),
        in_specs=[seq(3 * RWKV_WIDTH), seq(LORA_PAD), _const_spec(prev_rkv.shape),
                  _const_spec(prev_lora.shape), _const_spec(h0.shape)]
                 + [_const_spec(p.shape) for p in params],
        out_specs=[seq(RWKV_WIDTH),
                   pl.BlockSpec((1, N_PAIRS, PAIR, PAIR), lambda bi, t: (bi, 0, 0, 0))],
        out_shape=[jax.ShapeDtypeStruct((b, l, RWKV_WIDTH), BF16),
                   jax.ShapeDtypeStruct((b, N_PAIRS, PAIR, PAIR), F32)],
        scratch_shapes=[pltpu.VMEM((N_PAIRS, PAIR, PAIR), F32),
                        pltpu.VMEM((1, 3 * RWKV_WIDTH), F32), pltpu.VMEM((1, LORA_PAD), F32),
                        scr(), scr(), scr(), scr(), scr(), scr(), scr()],
        compiler_params=pltpu.CompilerParams(dimension_semantics=("parallel", "arbitrary"),
                                             vmem_limit_bytes=VMEM_LIMIT),
        name="rwkv7",
    )(rkv, lora, prev_rkv, prev_lora, h0, *params)


def _pad_rows(w, start, total):
    return jnp.pad(w, ((start, total - start - w.shape[0]), (0, 0)))


def kernel(x, meta_tokens, ffn1_norm, ffn1_w_gate, ffn1_w_up, ffn1_w_down, mix_norm, w_in, rwkv_mu, rwkv_w0, rwkv_w_up, rwkv_a0, rwkv_a_up, rwkv_g_up, rwkv_k_k, rwkv_k_a, rwkv_r_k, rwkv_lnx_w, rwkv_lnx_b, w_out, ffn2_norm, ffn2_w_gate, ffn2_w_up, ffn2_w_down, final_norm):
    b, l, _ = x.shape
    depth = ffn1_norm.shape[0]
    assert depth == 1, "stacked layers would need the meta rows carried through the mixers"
    row =lambda p: p.reshape(1, -1).astype(F32)
    n_rkv = 3 * RWKV_WIDTH
    n_lora = W_LORA + A_LORA + G_LORA

    h = x.reshape(b * l, D_MODEL)
    h_meta = _pad_rows(meta_tokens.astype(F32), META_ROWS - N_META, META_ROWS)
    for d in range(depth):
        w_qkv = w_in[d][:, :3 * SB_WIDTH]
        w_qkv = w_qkv.at[:, :SB_WIDTH].multiply(HEAD_DIM ** -0.5 * LOG2_E).astype(BF16)
        w_rkv = w_in[d][:, 3 * SB_WIDTH:3 * SB_WIDTH + n_rkv].astype(BF16)
        w_lora = jnp.pad(w_in[d][:, 3 * SB_WIDTH + n_rkv:],
                         ((0, 0), (0, LORA_PAD - n_lora))).astype(BF16)
        mu_rkv = row(rwkv_mu[d][:n_rkv])
        mu_lora = jnp.pad(row(rwkv_mu[d][n_rkv:]), ((0, 0), (0, LORA_PAD - n_lora)))
        rwkv_params = (
            mu_rkv, mu_lora, row(rwkv_w0[d]), _pad_rows(rwkv_w_up[d], 0, LORA_PAD),
            row(rwkv_a0[d]), _pad_rows(rwkv_a_up[d], W_LORA, LORA_PAD),
            _pad_rows(rwkv_g_up[d], W_LORA + A_LORA, LORA_PAD), row(rwkv_k_k[d]),
            row(rwkv_k_a[d]), row(rwkv_r_k[d]), row(rwkv_lnx_w[d]), row(rwkv_lnx_b[d]))
        ffn1_w = (row(ffn1_norm[d]), ffn1_w_gate[d].astype(BF16), ffn1_w_up[d].astype(BF16),
                  ffn1_w_down[d].astype(BF16))
        ffn2_w = (row(ffn2_norm[d]), ffn2_w_gate[d].astype(BF16), ffn2_w_up[d].astype(BF16),
                  ffn2_w_down[d].astype(BF16))
        wo_sb = w_out[d][:SB_WIDTH].astype(BF16)
        wo_rw = w_out[d][SB_WIDTH:].astype(BF16)
        gf = row(final_norm)

        h = _ffn1(h, *ffn1_w)
        h_meta = _ffn1(h_meta, *ffn1_w)
        qkv, rkv, lora = _in_proj(h, row(mix_norm[d]), w_qkv, w_rkv, w_lora)
        qkv_m, rkv_m, lora_m = _in_proj(h_meta, row(mix_norm[d]), w_qkv, w_rkv, w_lora)

        sb = _sb_attention(qkv.reshape(b, l, -1), qkv_m)

        zero_state = jnp.zeros((N_PAIRS, PAIR, PAIR), F32)
        _, h_meta_state = _rwkv(rkv_m[None], lora_m[None], jnp.zeros((1, n_rkv), F32),
                                jnp.zeros((1, LORA_PAD), F32), zero_state, rwkv_params)
        rw, _ = _rwkv(rkv.reshape(b, l, -1), lora.reshape(b, l, -1), rkv_m[-1:], lora_m[-1:],
                      h_meta_state[0], rwkv_params)

        h = _ffn2(h, sb.reshape(b * l, -1), rw.reshape(b * l, -1), wo_sb, wo_rw, *ffn2_w, gf)
    return h.reshape(b, l, D_MODEL)
```

```python
import functools

import jax
import jax.numpy as jnp
from jax import lax
from jax.experimental import pallas as pl
from jax.experimental.pallas import tpu as pltpu

D_MODEL = 1024
N_META = 16
HEAD_DIM = 64
SB_WIDTH = 512
RWKV_WIDTH = 512
D_FF = 2816
W_LORA = 32
A_LORA = 32
G_LORA = 96
RMS_EPS = 1e-6
LNX_EPS = 64e-5

LANES = 128
META_ROWS = 128
LORA_PAD = 256
CHUNK = 64
PAIR = 2 * HEAD_DIM
N_PAIRS = RWKV_WIDTH // PAIR
VMEM_LIMIT = 56 * 1024 * 1024
SB_Q_TILE = 1024
SB_K_TILE = 256

F32 = jnp.float32
BF16 = jnp.bfloat16
HI = lax.Precision.HIGHEST
LOG2_E = 1.4426950408889634
EXP2_CLAMP = 126.0


def _const_spec(shape):
    zeros = (0,) * len(shape)
    return pl.BlockSpec(shape, lambda *_: zeros, pipeline_mode=pl.Buffered(1))


def _rms(x, g):
    ms = jnp.mean(x * x, axis=-1, keepdims=True)
    return x * lax.rsqrt(ms + RMS_EPS) * g


def _dot(a, b, precision=None):
    return jnp.dot(a, b, preferred_element_type=F32, precision=precision)


def _dot_nt(a, b, precision=None):
    return lax.dot_general(a, b, (((1,), (1,)), ((), ())), preferred_element_type=F32,
                           precision=precision)


def _dot_tn(a, b, precision=None):
    return lax.dot_general(a, b, (((0,), (0,)), ((), ())), preferred_element_type=F32,
                           precision=precision)


def _swiglu_half_step(h, g, wg_ref, wu_ref, wd_ref):
    n = _rms(h, g).astype(BF16)
    gate = _dot(n, wg_ref[...])
    up = _dot(n, wu_ref[...])
    act = (gate * jax.nn.sigmoid(gate) * up).astype(BF16)
    return h + 0.5 * _dot(act, wd_ref[...])


def _ffn1_kernel(x_ref, g_ref, wg_ref, wu_ref, wd_ref, o_ref):
    o_ref[...] = _swiglu_half_step(x_ref[...], g_ref[...], wg_ref, wu_ref, wd_ref)


def _ffn2_kernel(h_ref, sb_ref, rw_ref, wo_sb_ref, wo_rw_ref, g_ref, wg_ref, wu_ref, wd_ref,
                 gf_ref, o_ref):
    h = h_ref[...] + _dot(sb_ref[...], wo_sb_ref[...]) + _dot(rw_ref[...], wo_rw_ref[...])
    h = _swiglu_half_step(h, g_ref[...], wg_ref, wu_ref, wd_ref)
    o_ref[...] = _rms(h, gf_ref[...])


def _in_proj_kernel(h_ref, g_ref, wqkv_ref, wrkv_ref, wlora_ref, qkv_ref, rkv_ref, lora_ref):
    n = _rms(h_ref[...], g_ref[...]).astype(BF16)
    qkv_ref[...] = _dot(n, wqkv_ref[...]).astype(BF16)
    rkv_ref[...] = _dot(n, wrkv_ref[...])
    lora_ref[...] = _dot(n, wlora_ref[...])


def _token_tile(n_tokens, want):
    t = min(want, n_tokens)
    assert n_tokens % t == 0 and t % 8 == 0, (n_tokens, t)
    return t


def _row_spec(t, width):
    return pl.BlockSpec((t, width), lambda i: (i, 0))


def _ffn1(x2d, g, wg, wu, wd):
    n = x2d.shape[0]
    t = _token_tile(n, 512)
    return pl.pallas_call(
        _ffn1_kernel,
        grid=(n // t,),
        in_specs=[_row_spec(t, D_MODEL), _const_spec((1, D_MODEL)), _const_spec(wg.shape),
                  _const_spec(wu.shape), _const_spec(wd.shape)],
        out_specs=_row_spec(t, D_MODEL),
        out_shape=jax.ShapeDtypeStruct((n, D_MODEL), F32),
        compiler_params=pltpu.CompilerParams(dimension_semantics=("parallel",),
                                             vmem_limit_bytes=VMEM_LIMIT),
        name="ffn1",
    )(x2d, g, wg, wu, wd)


def _ffn2(h2d, sb2d, rw2d, wo_sb, wo_rw, g, wg, wu, wd, gf):
    n = h2d.shape[0]
    t = _token_tile(n, 512)
    return pl.pallas_call(
        _ffn2_kernel,
        grid=(n // t,),
        in_specs=[_row_spec(t, D_MODEL), _row_spec(t, SB_WIDTH), _row_spec(t, RWKV_WIDTH),
                  _const_spec(wo_sb.shape), _const_spec(wo_rw.shape), _const_spec((1, D_MODEL)),
                  _const_spec(wg.shape), _const_spec(wu.shape), _const_spec(wd.shape),
                  _const_spec((1, D_MODEL))],
        out_specs=_row_spec(t, D_MODEL),
        out_shape=jax.ShapeDtypeStruct((n, D_MODEL), F32),
        compiler_params=pltpu.CompilerParams(dimension_semantics=("parallel",),
                                             vmem_limit_bytes=VMEM_LIMIT),
        name="ffn2",
    )(h2d, sb2d, rw2d, wo_sb, wo_rw, g, wg, wu, wd, gf)


def _in_proj(h2d, g, wqkv, wrkv, wlora):
    n = h2d.shape[0]
    t = _token_tile(n, 512)
    return pl.pallas_call(
        _in_proj_kernel,
        grid=(n // t,),
        in_specs=[_row_spec(t, D_MODEL), _const_spec((1, D_MODEL)), _const_spec(wqkv.shape),
                  _const_spec(wrkv.shape), _const_spec(wlora.shape)],
        out_specs=[_row_spec(t, 3 * SB_WIDTH), _row_spec(t, 3 * RWKV_WIDTH),
                   _row_spec(t, LORA_PAD)],
        out_shape=[jax.ShapeDtypeStruct((n, 3 * SB_WIDTH), BF16),
                   jax.ShapeDtypeStruct((n, 3 * RWKV_WIDTH), F32),
                   jax.ShapeDtypeStruct((n, LORA_PAD), F32)],
        compiler_params=pltpu.CompilerParams(dimension_semantics=("parallel",),
                                             vmem_limit_bytes=VMEM_LIMIT),
        name="in_proj",
    )(h2d, g, wqkv, wrkv, wlora)


def _sb_attn_kernel(q_ref, k_ref, v_ref, km_ref, vm_ref, o_ref, qh_sc, acc_sc, rest_sc, pvm_sc, *,
                    tq, tk):
    i = pl.program_id(2)
    q = q_ref[0]
    low = lax.broadcasted_iota(jnp.int32, (1, LANES), 1) < HEAD_DIM
    zero = jnp.zeros_like(q)
    qh_sc[0] = jnp.where(low, q, zero)
    qh_sc[1] = jnp.where(low, zero, q)
    acc_sc[...] = jnp.zeros_like(acc_sc)
    rest_sc[...] = jnp.zeros_like(rest_sc)

    row = lax.broadcasted_iota(jnp.int32, (tk, tk), 0)
    col = lax.broadcasted_iota(jnp.int32, (tk, tk), 1)
    suffix_w = jnp.where(row > col, -1.0, 0.0).astype(BF16)
    diag_mask = col < row
    meta_mask = lax.broadcasted_iota(jnp.int32, (1, META_ROWS), 1) >= META_ROWS - N_META

    def sweep(jobs):
        for r0, n, kb, vb, w, mask, apply in jobs:
            rows = slice(r0, r0 + n)
            for hh in range(2):
                z = _dot_nt(qh_sc[hh, rows, :], kb)
                sp = jnp.maximum(z, jnp.log2(1.0 + jnp.exp2(jnp.minimum(z, EXP2_CLAMP))))
                if mask is not None:
                    sp = jnp.where(mask, sp, 0.0)
                cum = _dot(sp.astype(BF16), w)
                p = jnp.exp2(z - sp + cum)
                if mask is not None:
                    p = jnp.where(mask, p, 0.0)
                pv = _dot(p.astype(BF16), vb)
                if apply:
                    rest = rest_sc[hh, rows, :]
                    acc_sc[hh, rows, :] += jnp.exp2(rest) * pv
                    rest_sc[hh, rows, :] = rest + (cum[:, :1] - sp[:, :1])
                else:
                    pvm_sc[hh, rows, :] = pv

    def key_tile(j):
        start = pl.multiple_of(j * tk, tk)
        return k_ref[0, pl.ds(start, tk), :], v_ref[0, pl.ds(start, tk), :]

    n_diag = tq // tk
    jobs = []
    for jj in reversed(range(n_diag)):
        kb, vb = key_tile(i * n_diag + jj)
        jobs.append((jj * tk, tk, kb, vb, suffix_w, diag_mask, True))
        if jj + 1 < n_diag:
            jobs.append(((jj + 1) * tk, tq - (jj + 1) * tk, kb, vb, suffix_w, None, True))
    jobs.append((0, tq, km_ref[...], vm_ref[...], suffix_w[tk - META_ROWS:, tk - META_ROWS:],
                 meta_mask, False))
    sweep(jobs)

    def body(t, _):
        sweep([(0, tq) + key_tile((i - 1 - t) * n_diag + jj) + (suffix_w, None, True)
               for jj in reversed(range(n_diag))])
        return 0

    lax.fori_loop(0, i, body, 0)
    for hh in range(2):
        acc_sc[hh] += jnp.exp2(rest_sc[hh]) * pvm_sc[hh]
    o_ref[0] = jnp.where(low, acc_sc[0], acc_sc[1]).astype(o_ref.dtype)


def _sb_attention(qkv, qkv_meta):
    b, l, _ = qkv.shape
    tq = min(SB_Q_TILE, l)
    tk = min(SB_K_TILE, l)
    assert l % tq == 0 and tq % tk == 0 and tk >= META_ROWS
    n_pairs = SB_WIDTH // LANES
    return pl.pallas_call(
        functools.partial(_sb_attn_kernel, tq=tq, tk=tk),
        grid=(b, n_pairs, l // tq),
        in_specs=[
            pl.BlockSpec((1, tq, LANES), lambda bi, hp, i: (bi, i, hp)),
            pl.BlockSpec((1, l, LANES), lambda bi, hp, i: (bi, 0, n_pairs + hp)),
            pl.BlockSpec((1, l, LANES), lambda bi, hp, i: (bi, 0, 2 * n_pairs + hp)),
            pl.BlockSpec((META_ROWS, LANES), lambda bi, hp, i: (0, n_pairs + hp)),
            pl.BlockSpec((META_ROWS, LANES), lambda bi, hp, i: (0, 2 * n_pairs + hp)),
        ],
        out_specs=pl.BlockSpec((1, tq, LANES), lambda bi, hp, i: (bi, i, hp)),
        out_shape=jax.ShapeDtypeStruct((b, l, SB_WIDTH), BF16),
        scratch_shapes=[pltpu.VMEM((2, tq, LANES), BF16), pltpu.VMEM((2, tq, LANES), F32),
                        pltpu.VMEM((2, tq, 1), F32), pltpu.VMEM((2, tq, LANES), F32)],
        compiler_params=pltpu.CompilerParams(
            dimension_semantics=("parallel", "parallel", "arbitrary"),
            vmem_limit_bytes=VMEM_LIMIT),
        name="sb_attention",
    )(qkv, qkv, qkv, qkv_meta, qkv_meta)


def _bdot(a, b):
    return _dot(a.astype(BF16), b.astype(BF16))


def _rwkv_kernel(rkv_ref, lora_ref, prev_rkv_ref, prev_lora_ref, h0_ref, mu_rkv_ref, mu_lora_ref,
                 w0_ref, wup_ref, a0_ref, aup_ref, gup_ref, kk_ref, ka_ref, rk_ref, lnw_ref,
                 lnb_ref, y_ref, hT_ref, h_sc, prev_rkv_sc, prev_lora_sc, r_sc, k_sc, v_sc,
                 kk_sc, a_sc, lw_sc, y_sc, *, tile, chunks_per_trip):
    c_w = RWKV_WIDTH

    @pl.when(pl.program_id(1) == 0)
    def _():
        h_sc[...] = h0_ref[...]
        prev_rkv_sc[...] = prev_rkv_ref[...]
        prev_lora_sc[...] = prev_lora_ref[...]

    first_row = lax.broadcasted_iota(jnp.int32, (tile, 1), 0) == 0

    def token_shift(p, prev_sc, mu):
        p_prev = jnp.where(first_row, prev_sc[...], pltpu.roll(p, 1, axis=0))
        prev_sc[...] = p[tile - 1:tile, :]
        return p + (p_prev - p) * mu

    p = token_shift(rkv_ref[0], prev_rkv_sc, mu_rkv_ref[...])
    lora = token_shift(lora_ref[0], prev_lora_sc, mu_lora_ref[...])
    r = p[:, :c_w]
    k = p[:, c_w:2 * c_w]
    v = p[:, 2 * c_w:]
    w = -jax.nn.softplus(-(w0_ref[...] + _bdot(jnp.tanh(lora), wup_ref[...]))) - 0.5
    a = jax.nn.sigmoid(a0_ref[...] + _bdot(lora, aup_ref[...]))
    gate = _bdot(jax.nn.sigmoid(lora), gup_ref[...])

    hr = lax.broadcasted_iota(jnp.int32, (c_w, c_w), 0) // HEAD_DIM
    hc = lax.broadcasted_iota(jnp.int32, (c_w, c_w), 1) // HEAD_DIM
    head_sum = jnp.where(hr == hc, 1.0, 0.0).astype(BF16)

    kk = k * kk_ref[...]
    kk = kk * lax.rsqrt(jnp.maximum(_bdot(kk * kk, head_sum), 1e-24))
    k = k * (1.0 + (a - 1.0) * ka_ref[...])
    r_sc[...] = r
    k_sc[...] = k
    v_sc[...] = v
    kk_sc[...] = kk
    a_sc[...] = a
    lw_sc[...] = -jnp.exp(w)

    rr = lax.broadcasted_iota(jnp.int32, (PAIR, PAIR), 0)
    cc = lax.broadcasted_iota(jnp.int32, (PAIR, PAIR), 1)
    same_head = (rr < HEAD_DIM) == (cc < HEAD_DIM)
    strict_lower = same_head & (cc < rr)
    lower = same_head & (cc <= rr)
    eye_mask = rr == cc
    eye = jnp.where(eye_mask, 1.0, 0.0)
    tr = lax.broadcasted_iota(jnp.int32, (CHUNK, CHUNK), 0)
    tc = lax.broadcasted_iota(jnp.int32, (CHUNK, CHUNK), 1)
    cumsum_w = jnp.where(tc <= tr, 1.0, 0.0).astype(BF16)

    zero = jnp.zeros((PAIR, PAIR), BF16)

    def chunk_operands(c):
        rows = pl.ds(pl.multiple_of(c * CHUNK, CHUNK), CHUNK)
        lw = lw_sc[rows, :]
        lw_hi = lw.astype(BF16)
        g = _dot(cumsum_w, lw_hi) + _dot(cumsum_w, (lw - lw_hi.astype(F32)).astype(BF16))
        g_end = g[CHUNK - 1:CHUNK, :]
        decay_out = jnp.exp(-g)
        decay_end = jnp.exp(g_end - g)
        kk_c = kk_sc[rows, :]
        kka = kk_c * a_sc[rows, :]
        k_c = k_sc[rows, :]
        full = dict(
            a_bar=(-kk_c * jnp.exp(g - lw)).astype(BF16),
            r_bar=(r_sc[rows, :] * jnp.exp(g)).astype(BF16),
            b_til=(kka * decay_out).astype(BF16), k_til=(k_c * decay_out).astype(BF16),
            b_hat=(kka * decay_end).astype(BF16), k_hat=(k_c * decay_end).astype(BF16),
            v=v_sc[rows, :].astype(BF16))
        gamma_end = jnp.exp(g_end)
        masked = ("a_bar", "r_bar", "v", "b_hat", "k_hat")
        out = []
        for pr in range(N_PAIRS):
            sl = slice(PAIR * pr, PAIR * (pr + 1))
            ops = {}
            for name, x in full.items():
                x2 = jnp.concatenate([x[:, sl], x[:, sl]], axis=0)
                ops[name] = jnp.where(same_head, x2, zero) if name in masked else x2
            ops.update(rows=rows, sl=sl, pr=pr, gamma_end=gamma_end[:, sl])
            out.append(ops)
        return out

    def trip(t, _):
        chains = [ops for step in range(chunks_per_trip)
                  for ops in chunk_operands(t * chunks_per_trip + step)]
        s1 = [_dot_nt(jnp.concatenate([o["a_bar"], o["r_bar"]], axis=0),
                      jnp.concatenate([o["b_til"], o["k_til"]], axis=0)) for o in chains]
        n_mat = [jnp.where(strict_lower, s[:PAIR, :PAIR], 0.0) for s in s1]
        a_ak = [jnp.where(strict_lower, s[:PAIR, PAIR:], 0.0).astype(BF16) for s in s1]
        a_rb = [jnp.where(lower, s[PAIR:, :PAIR], 0.0).astype(BF16) for s in s1]
        a_rk = [jnp.where(lower, s[PAIR:, PAIR:], 0.0).astype(BF16) for s in s1]
        inv = [eye + n for n in n_mat]
        power = [n.astype(BF16) for n in n_mat]
        for _ in range(5):
            power = [_dot(p, p).astype(BF16) for p in power]
            inv = [i + _dot(i.astype(BF16), p) for i, p in zip(inv, power)]
        akv = [_dot(a, o["v"]).astype(BF16) for a, o in zip(a_ak, chains)]
        pq = [_dot(i.astype(BF16), jnp.concatenate([o["a_bar"], x], axis=1)).astype(BF16)
              for i, o, x in zip(inv, chains, akv)]
        rkv = [_dot(a, o["v"]) for a, o in zip(a_rk, chains)]
        ef = [_dot(a, x) + jnp.concatenate([o["r_bar"].astype(F32), y], axis=1)
              for a, x, o, y in zip(a_rb, pq, chains, rkv)]
        ef = [x[:CHUNK] + x[CHUNK:] for x in ef]
        mg = [_dot_tn(o["b_hat"], x) for o, x in zip(chains, pq)]
        kv = [_dot_tn(o["k_hat"], o["v"]) for o in chains]
        for o, x, m, g2 in zip(chains, ef, mg, kv):
            m_mat = m[:, :PAIR] + jnp.where(eye_mask, o["gamma_end"], 0.0)
            eh = _bdot(jnp.concatenate([x[:, :PAIR], m_mat], axis=0), h_sc[o["pr"]])
            y_sc[o["rows"], o["sl"]] = eh[:CHUNK] + x[:, PAIR:]
            h_sc[o["pr"]] = eh[CHUNK:] + (m[:, PAIR:] + g2)
        return 0

    lax.fori_loop(0, tile // (CHUNK * chunks_per_trip), trip, 0)

    y = y_sc[...]
    inv_n = 1.0 / HEAD_DIM
    d = y - _bdot(y, head_sum) * inv_n
    yn = d * lax.rsqrt(_bdot(d * d, head_sum) * inv_n + LNX_EPS)
    yn = yn * lnw_ref[...] + lnb_ref[...]
    bonus = _bdot(r_sc[...] * k_sc[...] * rk_ref[...], head_sum) * v_sc[...]
    y_ref[0] = ((yn + bonus) * gate).astype(y_ref.dtype)
    hT_ref[0] = h_sc[...]


def _rwkv(rkv, lora, prev_rkv, prev_lora, h0, params):
    b, l, _ = rkv.shape
    tile = min(512, l)
    chunks_per_trip = 2
    assert l % tile == 0 and tile % (CHUNK * chunks_per_trip) == 0
    seq = lambda w: pl.BlockSpec((1, tile, w), lambda bi, t: (bi, t, 0))
    scr = lambda: pltpu.VMEM((tile, RWKV_WIDTH), F32)
    return pl.pallas_call(
        functools.partial(_rwkv_kernel, tile=tile, chunks_per_trip=chunks_per_trip),
        grid=(b, l // tile),
        in_specs=[seq(3 * RWKV_WIDTH), seq(LORA_PAD), _const_spec(prev_rkv.shape),
                  _const_spec(prev_lora.shape), _const_spec(h0.shape)]
                 + [_const_spec(p.shape) for p in params],
        out_specs=[seq(RWKV_WIDTH),
                   pl.BlockSpec((1, N_PAIRS, PAIR, PAIR), lambda bi, t: (bi, 0, 0, 0))],
        out_shape=[jax.ShapeDtypeStruct((b, l, RWKV_WIDTH), BF16),
                   jax.ShapeDtypeStruct((b, N_PAIRS, PAIR, PAIR), F32)],
        scratch_shapes=[pltpu.VMEM((N_PAIRS, PAIR, PAIR), F32),
                        pltpu.VMEM((1, 3 * RWKV_WIDTH), F32), pltpu.VMEM((1, LORA_PAD), F32),
                        scr(), scr(), scr(), scr(), scr(), scr(), scr()],
        compiler_params=pltpu.CompilerParams(dimension_semantics=("parallel", "arbitrary"),
                                             vmem_limit_bytes=VMEM_LIMIT),
        name="rwkv7",
    )(rkv, lora, prev_rkv, prev_lora, h0, *params)


def _pad_rows(w, start, total):
    return jnp.pad(w, ((start, total - start - w.shape[0]), (0, 0)))


def kernel(x, meta_tokens, ffn1_norm, ffn1_w_gate, ffn1_w_up, ffn1_w_down, mix_norm, w_in, rwkv_mu, rwkv_w0, rwkv_w_up, rwkv_a0, rwkv_a_up, rwkv_g_up, rwkv_k_k, rwkv_k_a, rwkv_r_k, rwkv_lnx_w, rwkv_lnx_b, w_out, ffn2_norm, ffn2_w_gate, ffn2_w_up, ffn2_w_down, final_norm):
    b, l, _ = x.shape
    depth = ffn1_norm.shape[0]
    assert depth == 1, "stacked layers would need the meta rows carried through the mixers"
    row =lambda p: p.reshape(1, -1).astype(F32)
    n_rkv = 3 * RWKV_WIDTH
    n_lora = W_LORA + A_LORA + G_LORA

    h = x.reshape(b * l, D_MODEL)
    h_meta = _pad_rows(meta_tokens.astype(F32), META_ROWS - N_META, META_ROWS)
    for d in range(depth):
        w_qkv = w_in[d][:, :3 * SB_WIDTH]
        w_qkv = w_qkv.at[:, :SB_WIDTH].multiply(HEAD_DIM ** -0.5 * LOG2_E).astype(BF16)
        w_rkv = w_in[d][:, 3 * SB_WIDTH:3 * SB_WIDTH + n_rkv].astype(BF16)
        w_lora = jnp.pad(w_in[d][:, 3 * SB_WIDTH + n_rkv:],
                         ((0, 0), (0, LORA_PAD - n_lora))).astype(BF16)
        mu_rkv = row(rwkv_mu[d][:n_rkv])
        mu_lora = jnp.pad(row(rwkv_mu[d][n_rkv:]), ((0, 0), (0, LORA_PAD - n_lora)))
        rwkv_params = (
            mu_rkv, mu_lora, row(rwkv_w0[d]), _pad_rows(rwkv_w_up[d], 0, LORA_PAD),
            row(rwkv_a0[d]), _pad_rows(rwkv_a_up[d], W_LORA, LORA_PAD),
            _pad_rows(rwkv_g_up[d], W_LORA + A_LORA, LORA_PAD), row(rwkv_k_k[d]),
            row(rwkv_k_a[d]), row(rwkv_r_k[d]), row(rwkv_lnx_w[d]), row(rwkv_lnx_b[d]))
        ffn1_w = (row(ffn1_norm[d]), ffn1_w_gate[d].astype(BF16), ffn1_w_up[d].astype(BF16),
                  ffn1_w_down[d].astype(BF16))
        ffn2_w = (row(ffn2_norm[d]), ffn2_w_gate[d].astype(BF16), ffn2_w_up[d].astype(BF16),
                  ffn2_w_down[d].astype(BF16))
        wo_sb = w_out[d][:SB_WIDTH].astype(BF16)
        wo_rw = w_out[d][SB_WIDTH:].astype(BF16)
        gf = row(final_norm)

        h = _ffn1(h, *ffn1_w)
        h_meta = _ffn1(h_meta, *ffn1_w)
        qkv, rkv, lora = _in_proj(h, row(mix_norm[d]), w_qkv, w_rkv, w_lora)
        qkv_m, rkv_m, lora_m = _in_proj(h_meta, row(mix_norm[d]), w_qkv, w_rkv, w_lora)

        sb = _sb_attention(qkv.reshape(b, l, -1), qkv_m)

        zero_state = jnp.zeros((N_PAIRS, PAIR, PAIR), F32)
        _, h_meta_state = _rwkv(rkv_m[None], lora_m[None], jnp.zeros((1, n_rkv), F32),
                                jnp.zeros((1, LORA_PAD), F32), zero_state, rwkv_params)
        rw, _ = _rwkv(rkv.reshape(b, l, -1), lora.reshape(b, l, -1), rkv_m[-1:], lora_m[-1:],
                      h_meta_state[0], rwkv_params)

        h = _ffn2(h, sb.reshape(b * l, -1), rw.reshape(b * l, -1), wo_sb, wo_rw, *ffn2_w, gf)
    return h.reshape(b, l, D_MODEL)
```

```python
import functools

import jax
import jax.numpy as jnp
from jax import lax
from jax.experimental import pallas as pl
from jax.experimental.pallas import tpu as pltpu

D_MODEL = 1024
N_META = 16
HEAD_DIM = 64
SB_WIDTH = 512
RWKV_WIDTH = 512
D_FF = 2816
W_LORA = 32
A_LORA = 32
G_LORA = 96
RMS_EPS = 1e-6
LNX_EPS = 64e-5

LANES = 128
META_ROWS = 128
LORA_PAD = 256
CHUNK = 64
PAIR = 2 * HEAD_DIM
N_PAIRS = RWKV_WIDTH // PAIR
VMEM_LIMIT = 56 * 1024 * 1024
SB_Q_TILE = 1024
SB_K_TILE = 256

F32 = jnp.float32
BF16 = jnp.bfloat16
HI = lax.Precision.HIGHEST
LOG2_E = 1.4426950408889634
EXP2_CLAMP = 126.0
DEAD_LOG2 = -150.0


def _const_spec(shape):
    zeros = (0,) * len(shape)
    return pl.BlockSpec(shape, lambda *_: zeros, pipeline_mode=pl.Buffered(1))


def _rms(x, g):
    ms = jnp.mean(x * x, axis=-1, keepdims=True)
    return x * lax.rsqrt(ms + RMS_EPS) * g


def _dot(a, b, precision=None):
    return jnp.dot(a, b, preferred_element_type=F32, precision=precision)


def _dot_nt(a, b, precision=None):
    return lax.dot_general(a, b, (((1,), (1,)), ((), ())), preferred_element_type=F32,
                           precision=precision)


def _dot_tn(a, b, precision=None):
    return lax.dot_general(a, b, (((0,), (0,)), ((), ())), preferred_element_type=F32,
                           precision=precision)


def _swiglu_half_step(h, g, wg_ref, wu_ref, wd_ref):
    n = _rms(h, g).astype(BF16)
    gate = _dot(n, wg_ref[...])
    up = _dot(n, wu_ref[...])
    act = (gate * jax.nn.sigmoid(gate) * up).astype(BF16)
    return h + 0.5 * _dot(act, wd_ref[...])


def _ffn1_kernel(x_ref, g_ref, wg_ref, wu_ref, wd_ref, o_ref):
    o_ref[...] = _swiglu_half_step(x_ref[...], g_ref[...], wg_ref, wu_ref, wd_ref)


def _ffn2_kernel(h_ref, sb_ref, rw_ref, wo_sb_ref, wo_rw_ref, g_ref, wg_ref, wu_ref, wd_ref,
                 gf_ref, o_ref):
    h = h_ref[...] + _dot(sb_ref[...], wo_sb_ref[...]) + _dot(rw_ref[...], wo_rw_ref[...])
    h = _swiglu_half_step(h, g_ref[...], wg_ref, wu_ref, wd_ref)
    o_ref[...] = _rms(h, gf_ref[...])


def _in_proj_kernel(h_ref, g_ref, wqkv_ref, wrkv_ref, wlora_ref, qkv_ref, rkv_ref, lora_ref):
    n = _rms(h_ref[...], g_ref[...]).astype(BF16)
    qkv_ref[...] = _dot(n, wqkv_ref[...]).astype(BF16)
    rkv_ref[...] = _dot(n, wrkv_ref[...])
    lora_ref[...] = _dot(n, wlora_ref[...])


def _token_tile(n_tokens, want):
    t = min(want, n_tokens)
    assert n_tokens % t == 0 and t % 8 == 0, (n_tokens, t)
    return t


def _row_spec(t, width):
    return pl.BlockSpec((t, width), lambda i: (i, 0))


def _ffn1(x2d, g, wg, wu, wd):
    n = x2d.shape[0]
    t = _token_tile(n, 512)
    return pl.pallas_call(
        _ffn1_kernel,
        grid=(n // t,),
        in_specs=[_row_spec(t, D_MODEL), _const_spec((1, D_MODEL)), _const_spec(wg.shape),
                  _const_spec(wu.shape), _const_spec(wd.shape)],
        out_specs=_row_spec(t, D_MODEL),
        out_shape=jax.ShapeDtypeStruct((n, D_MODEL), F32),
        compiler_params=pltpu.CompilerParams(dimension_semantics=("parallel",),
                                             vmem_limit_bytes=VMEM_LIMIT),
        name="ffn1",
    )(x2d, g, wg, wu, wd)


def _ffn2(h2d, sb2d, rw2d, wo_sb, wo_rw, g, wg, wu, wd, gf):
    n = h2d.shape[0]
    t = _token_tile(n, 512)
    return pl.pallas_call(
        _ffn2_kernel,
        grid=(n // t,),
        in_specs=[_row_spec(t, D_MODEL), _row_spec(t, SB_WIDTH), _row_spec(t, RWKV_WIDTH),
                  _const_spec(wo_sb.shape), _const_spec(wo_rw.shape), _const_spec((1, D_MODEL)),
                  _const_spec(wg.shape), _const_spec(wu.shape), _const_spec(wd.shape),
                  _const_spec((1, D_MODEL))],
        out_specs=_row_spec(t, D_MODEL),
        out_shape=jax.ShapeDtypeStruct((n, D_MODEL), F32),
        compiler_params=pltpu.CompilerParams(dimension_semantics=("parallel",),
                                             vmem_limit_bytes=VMEM_LIMIT),
        name="ffn2",
    )(h2d, sb2d, rw2d, wo_sb, wo_rw, g, wg, wu, wd, gf)


def _in_proj(h2d, g, wqkv, wrkv, wlora):
    n = h2d.shape[0]
    t = _token_tile(n, 512)
    return pl.pallas_call(
        _in_proj_kernel,
        grid=(n // t,),
        in_specs=[_row_spec(t, D_MODEL), _const_spec((1, D_MODEL)), _const_spec(wqkv.shape),
                  _const_spec(wrkv.shape), _const_spec(wlora.shape)],
        out_specs=[_row_spec(t, 3 * SB_WIDTH), _row_spec(t, 3 * RWKV_WIDTH),
                   _row_spec(t, LORA_PAD)],
        out_shape=[jax.ShapeDtypeStruct((n, 3 * SB_WIDTH), BF16),
                   jax.ShapeDtypeStruct((n, 3 * RWKV_WIDTH), F32),
                   jax.ShapeDtypeStruct((n, LORA_PAD), F32)],
        compiler_params=pltpu.CompilerParams(dimension_semantics=("parallel",),
                                             vmem_limit_bytes=VMEM_LIMIT),
        name="in_proj",
    )(h2d, g, wqkv, wrkv, wlora)


def _sb_attn_kernel(q_ref, k_ref, v_ref, km_ref, vm_ref, o_ref, qh_sc, acc_sc, rest_sc, *, tq, tk):
    i = pl.program_id(2)
    q = q_ref[0]
    low = lax.broadcasted_iota(jnp.int32, (1, LANES), 1) < HEAD_DIM
    zero = jnp.zeros_like(q)
    qh_sc[0] = jnp.where(low, q, zero)
    qh_sc[1] = jnp.where(low, zero, q)
    acc_sc[...] = jnp.zeros_like(acc_sc)
    rest_sc[...] = jnp.zeros_like(rest_sc)

    row = lax.broadcasted_iota(jnp.int32, (tk, tk), 0)
    col = lax.broadcasted_iota(jnp.int32, (tk, tk), 1)
    suffix_w = jnp.where(row > col, -1.0, 0.0).astype(BF16)
    diag_mask = col < row
    meta_mask = lax.broadcasted_iota(jnp.int32, (1, META_ROWS), 1) >= META_ROWS - N_META

    def chain(hh, rows, kb, vb, w, mask, valid):
        z = _dot_nt(qh_sc[hh, rows, :], kb)
        sp = jnp.maximum(z, jnp.log2(1.0 + jnp.exp2(jnp.minimum(z, EXP2_CLAMP))))
        if mask is not None:
            sp = jnp.where(mask, sp, 0.0)
        cum = _dot(sp.astype(BF16), w)
        p = jnp.exp2(z - sp + cum)
        if mask is not None:
            p = jnp.where(mask, p, 0.0)
        pv = _dot(p.astype(BF16), vb)
        rest = rest_sc[hh, rows, :]
        scale = jnp.exp2(rest)
        total = cum[:, :1] - sp[:, :1]
        if valid is not None:
            scale = scale * valid
            total = total * valid
        acc_sc[hh, rows, :] += scale * pv
        rest_sc[hh, rows, :] = rest + total

    n_blk = tq // tk

    def wave(s, mask):
        for rb in range(n_blk):
            j = i * n_blk + rb - s
            valid = None if mask is not None else (j >= 0).astype(F32)
            start = pl.multiple_of(jnp.maximum(j, 0) * tk, tk)
            kb = k_ref[0, pl.ds(start, tk), :]
            vb = v_ref[0, pl.ds(start, tk), :]
            for hh in range(2):
                chain(hh, slice(rb * tk, (rb + 1) * tk), kb, vb, suffix_w, mask, valid)

    def alive():
        return jnp.maximum(jnp.max(rest_sc[0]), jnp.max(rest_sc[1])) > DEAD_LOG2

    wave(0, diag_mask)
    last = i * n_blk + n_blk - 1
    lax.while_loop(lambda c: (c[0] <= last) & c[1],
                   lambda c: (wave(c[0], None), (c[0] + 1, alive()))[1],
                   (jnp.int32(1), alive()))

    @pl.when(alive())
    def _():
        for hh in range(2):
            chain(hh, slice(0, tq), km_ref[...], vm_ref[...],
                  suffix_w[tk - META_ROWS:, tk - META_ROWS:], meta_mask, None)

    o_ref[0] = jnp.where(low, acc_sc[0], acc_sc[1]).astype(o_ref.dtype)


def _sb_attention(qkv, qkv_meta):
    b, l, _ = qkv.shape
    tq = min(SB_Q_TILE, l)
    tk = min(SB_K_TILE, l)
    assert l % tq == 0 and tq % tk == 0 and tk >= META_ROWS
    n_pairs = SB_WIDTH // LANES
    return pl.pallas_call(
        functools.partial(_sb_attn_kernel, tq=tq, tk=tk),
        grid=(b, n_pairs, l // tq),
        in_specs=[
            pl.BlockSpec((1, tq, LANES), lambda bi, hp, i: (bi, i, hp)),
            pl.BlockSpec((1, l, LANES), lambda bi, hp, i: (bi, 0, n_pairs + hp)),
            pl.BlockSpec((1, l, LANES), lambda bi, hp, i: (bi, 0, 2 * n_pairs + hp)),
            pl.BlockSpec((META_ROWS, LANES), lambda bi, hp, i: (0, n_pairs + hp)),
            pl.BlockSpec((META_ROWS, LANES), lambda bi, hp, i: (0, 2 * n_pairs + hp)),
        ],
        out_specs=pl.BlockSpec((1, tq, LANES), lambda bi, hp, i: (bi, i, hp)),
        out_shape=jax.ShapeDtypeStruct((b, l, SB_WIDTH), BF16),
        scratch_shapes=[pltpu.VMEM((2, tq, LANES), BF16), pltpu.VMEM((2, tq, LANES), F32),
                        pltpu.VMEM((2, tq, 1), F32)],
        compiler_params=pltpu.CompilerParams(
            dimension_semantics=("parallel", "parallel", "arbitrary"),
            vmem_limit_bytes=VMEM_LIMIT),
        name="sb_attention",
    )(qkv, qkv, qkv, qkv_meta, qkv_meta)


def _bdot(a, b):
    return _dot(a.astype(BF16), b.astype(BF16))


def _rwkv_kernel(rkv_ref, lora_ref, prev_rkv_ref, prev_lora_ref, h0_ref, mu_rkv_ref, mu_lora_ref,
                 w0_ref, wup_ref, a0_ref, aup_ref, gup_ref, kk_ref, ka_ref, rk_ref, lnw_ref,
                 lnb_ref, y_ref, hT_ref, h_sc, prev_rkv_sc, prev_lora_sc, r_sc, k_sc, v_sc,
                 kk_sc, a_sc, lw_sc, y_sc, *, tile, chunks_per_trip):
    c_w = RWKV_WIDTH

    @pl.when(pl.program_id(1) == 0)
    def _():
        h_sc[...] = h0_ref[...]
        prev_rkv_sc[...] = prev_rkv_ref[...]
        prev_lora_sc[...] = prev_lora_ref[...]

    first_row = lax.broadcasted_iota(jnp.int32, (tile, 1), 0) == 0

    def token_shift(p, prev_sc, mu):
        p_prev = jnp.where(first_row, prev_sc[...], pltpu.roll(p, 1, axis=0))
        prev_sc[...] = p[tile - 1:tile, :]
        return p + (p_prev - p) * mu

    p = token_shift(rkv_ref[0], prev_rkv_sc, mu_rkv_ref[...])
    lora = token_shift(lora_ref[0], prev_lora_sc, mu_lora_ref[...])
    r = p[:, :c_w]
    k = p[:, c_w:2 * c_w]
    v = p[:, 2 * c_w:]
    w = -jax.nn.softplus(-(w0_ref[...] + _bdot(jnp.tanh(lora), wup_ref[...]))) - 0.5
    a = jax.nn.sigmoid(a0_ref[...] + _bdot(lora, aup_ref[...]))
    gate = _bdot(jax.nn.sigmoid(lora), gup_ref[...])

    hr = lax.broadcasted_iota(jnp.int32, (c_w, c_w), 0) // HEAD_DIM
    hc = lax.broadcasted_iota(jnp.int32, (c_w, c_w), 1) // HEAD_DIM
    head_sum = jnp.where(hr == hc, 1.0, 0.0).astype(BF16)

    kk = k * kk_ref[...]
    kk = kk * lax.rsqrt(jnp.maximum(_bdot(kk * kk, head_sum), 1e-24))
    k = k * (1.0 + (a - 1.0) * ka_ref[...])
    r_sc[...] = r
    k_sc[...] = k
    v_sc[...] = v
    kk_sc[...] = kk
    a_sc[...] = a
    lw_sc[...] = -jnp.exp(w)

    rr = lax.broadcasted_iota(jnp.int32, (PAIR, PAIR), 0)
    cc = lax.broadcasted_iota(jnp.int32, (PAIR, PAIR), 1)
    same_head = (rr < HEAD_DIM) == (cc < HEAD_DIM)
    strict_lower = same_head & (cc < rr)
    lower = same_head & (cc <= rr)
    eye_mask = rr == cc
    eye = jnp.where(eye_mask, 1.0, 0.0)
    tr = lax.broadcasted_iota(jnp.int32, (CHUNK, CHUNK), 0)
    tc = lax.broadcasted_iota(jnp.int32, (CHUNK, CHUNK), 1)
    cumsum_w = jnp.where(tc <= tr, 1.0, 0.0).astype(BF16)

    zero = jnp.zeros((PAIR, PAIR), BF16)

    def chunk_operands(c):
        rows = pl.ds(pl.multiple_of(c * CHUNK, CHUNK), CHUNK)
        lw = lw_sc[rows, :]
        lw_hi = lw.astype(BF16)
        g = _dot(cumsum_w, lw_hi) + _dot(cumsum_w, (lw - lw_hi.astype(F32)).astype(BF16))
        g_end = g[CHUNK - 1:CHUNK, :]
        decay_out = jnp.exp(-g)
        decay_end = jnp.exp(g_end - g)
        kk_c = kk_sc[rows, :]
        kka = kk_c * a_sc[rows, :]
        k_c = k_sc[rows, :]
        full = dict(
            a_bar=(-kk_c * jnp.exp(g - lw)).astype(BF16),
            r_bar=(r_sc[rows, :] * jnp.exp(g)).astype(BF16),
            b_til=(kka * decay_out).astype(BF16), k_til=(k_c * decay_out).astype(BF16),
            b_hat=(kka * decay_end).astype(BF16), k_hat=(k_c * decay_end).astype(BF16),
            v=v_sc[rows, :].astype(BF16))
        gamma_end = jnp.exp(g_end)
        masked = ("a_bar", "r_bar", "v", "b_hat", "k_hat")
        out = []
        for pr in range(N_PAIRS):
            sl = slice(PAIR * pr, PAIR * (pr + 1))
            ops = {}
            for name, x in full.items():
                x2 = jnp.concatenate([x[:, sl], x[:, sl]], axis=0)
                ops[name] = jnp.where(same_head, x2, zero) if name in masked else x2
            ops.update(rows=rows, sl=sl, pr=pr, gamma_end=gamma_end[:, sl])
            out.append(ops)
        return out

    def trip(t, _):
        chains = [ops for step in range(chunks_per_trip)
                  for ops in chunk_operands(t * chunks_per_trip + step)]
        s1 = [_dot_nt(jnp.concatenate([o["a_bar"], o["r_bar"]], axis=0),
                      jnp.concatenate([o["b_til"], o["k_til"]], axis=0)) for o in chains]
        n_mat = [jnp.where(strict_lower, s[:PAIR, :PAIR], 0.0) for s in s1]
        a_ak = [jnp.where(strict_lower, s[:PAIR, PAIR:], 0.0).astype(BF16) for s in s1]
        a_rb = [jnp.where(lower, s[PAIR:, :PAIR], 0.0).astype(BF16) for s in s1]
        a_rk = [jnp.where(lower, s[PAIR:, PAIR:], 0.0).astype(BF16) for s in s1]
        inv = [eye + n for n in n_mat]
        power = [n.astype(BF16) for n in n_mat]
        for _ in range(5):
            power = [_dot(p, p).astype(BF16) for p in power]
            inv = [i + _dot(i.astype(BF16), p) for i, p in zip(inv, power)]
        akv = [_dot(a, o["v"]).astype(BF16) for a, o in zip(a_ak, chains)]
        pq = [_dot(i.astype(BF16), jnp.concatenate([o["a_bar"], x], axis=1)).astype(BF16)
              for i, o, x in zip(inv, chains, akv)]
        rkv = [_dot(a, o["v"]) for a, o in zip(a_rk, chains)]
        ef = [_dot(a, x) + jnp.concatenate([o["r_bar"].astype(F32), y], axis=1)
              for a, x, o, y in zip(a_rb, pq, chains, rkv)]
        ef = [x[:CHUNK] + x[CHUNK:] for x in ef]
        mg = [_dot_tn(o["b_hat"], x) for o, x in zip(chains, pq)]
        kv = [_dot_tn(o["k_hat"], o["v"]) for o in chains]
        for o, x, m, g2 in zip(chains, ef, mg, kv):
            m_mat = m[:, :PAIR] + jnp.where(eye_mask, o["gamma_end"], 0.0)
            eh = _bdot(jnp.concatenate([x[:, :PAIR], m_mat], axis=0), h_sc[o["pr"]])
            y_sc[o["rows"], o["sl"]] = eh[:CHUNK] + x[:, PAIR:]
            h_sc[o["pr"]] = eh[CHUNK:] + (m[:, PAIR:] + g2)
        return 0

    lax.fori_loop(0, tile // (CHUNK * chunks_per_trip), trip, 0)

    y = y_sc[...]
    inv_n = 1.0 / HEAD_DIM
    d = y - _bdot(y, head_sum) * inv_n
    yn = d * lax.rsqrt(_bdot(d * d, head_sum) * inv_n + LNX_EPS)
    yn = yn * lnw_ref[...] + lnb_ref[...]
    bonus = _bdot(r_sc[...] * k_sc[...] * rk_ref[...], head_sum) * v_sc[...]
    y_ref[0] = ((yn + bonus) * gate).astype(y_ref.dtype)
    hT_ref[0] = h_sc[...]


def _rwkv(rkv, lora, prev_rkv, prev_lora, h0, params):
    b, l, _ = rkv.shape
    tile = min(512, l)
    chunks_per_trip = 2
    assert l % tile == 0 and tile % (CHUNK * chunks_per_trip) == 0
    seq = lambda w: pl.BlockSpec((1, tile, w), lambda bi, t: (bi, t, 0))
    scr = lambda: pltpu.VMEM((tile, RWKV_WIDTH), F32)
    return pl.pallas_call(
        functools.partial(_rwkv_kernel, tile=tile, chunks_per_trip=chunks_per_trip),
        grid=(b, l // tile),
        in_specs=[seq(3 * RWKV_WIDTH), seq(LORA_PAD), _const_spec(prev_rkv.shape),
                  _const_spec(prev_lora.shape), _const_spec(h0.shape)]
                 + [_const_spec(p.shape) for p in params],
        out_specs=[seq(RWKV_WIDTH),
                   pl.BlockSpec((1, N_PAIRS, PAIR, PAIR), lambda bi, t: (bi, 0, 0, 0))],
        out_shape=[jax.ShapeDtypeStruct((b, l, RWKV_WIDTH), BF16),
                   jax.ShapeDtypeStruct((b, N_PAIRS, PAIR, PAIR), F32)],
        scratch_shapes=[pltpu.VMEM((N_PAIRS, PAIR, PAIR), F32),
                        pltpu.VMEM((1, 3 * RWKV_WIDTH), F32), pltpu.VMEM((1, LORA_PAD), F32),
                        scr(), scr(), scr(), scr(), scr(), scr(), scr()],
        compiler_params=pltpu.CompilerParams(dimension_semantics=("parallel", "arbitrary"),
                                             vmem_limit_bytes=VMEM_LIMIT),
        name="rwkv7",
    )(rkv, lora, prev_rkv, prev_lora, h0, *params)


def _pad_rows(w, start, total):
    return jnp.pad(w, ((start, total - start - w.shape[0]), (0, 0)))


def kernel(x, meta_tokens, ffn1_norm, ffn1_w_gate, ffn1_w_up, ffn1_w_down, mix_norm, w_in, rwkv_mu, rwkv_w0, rwkv_w_up, rwkv_a0, rwkv_a_up, rwkv_g_up, rwkv_k_k, rwkv_k_a, rwkv_r_k, rwkv_lnx_w, rwkv_lnx_b, w_out, ffn2_norm, ffn2_w_gate, ffn2_w_up, ffn2_w_down, final_norm):
    b, l, _ = x.shape
    depth = ffn1_norm.shape[0]
    assert depth == 1, "stacked layers would need the meta rows carried through the mixers"
    row =lambda p: p.reshape(1, -1).astype(F32)
    n_rkv = 3 * RWKV_WIDTH
    n_lora = W_LORA + A_LORA + G_LORA

    h = x.reshape(b * l, D_MODEL)
    h_meta = _pad_rows(meta_tokens.astype(F32), META_ROWS - N_META, META_ROWS)
    for d in range(depth):
        w_qkv = w_in[d][:, :3 * SB_WIDTH]
        w_qkv = w_qkv.at[:, :SB_WIDTH].multiply(HEAD_DIM ** -0.5 * LOG2_E).astype(BF16)
        w_rkv = w_in[d][:, 3 * SB_WIDTH:3 * SB_WIDTH + n_rkv].astype(BF16)
        w_lora = jnp.pad(w_in[d][:, 3 * SB_WIDTH + n_rkv:],
                         ((0, 0), (0, LORA_PAD - n_lora))).astype(BF16)
        mu_rkv = row(rwkv_mu[d][:n_rkv])
        mu_lora = jnp.pad(row(rwkv_mu[d][n_rkv:]), ((0, 0), (0, LORA_PAD - n_lora)))
        rwkv_params = (
            mu_rkv, mu_lora, row(rwkv_w0[d]), _pad_rows(rwkv_w_up[d], 0, LORA_PAD),
            row(rwkv_a0[d]), _pad_rows(rwkv_a_up[d], W_LORA, LORA_PAD),
            _pad_rows(rwkv_g_up[d], W_LORA + A_LORA, LORA_PAD), row(rwkv_k_k[d]),
            row(rwkv_k_a[d]), row(rwkv_r_k[d]), row(rwkv_lnx_w[d]), row(rwkv_lnx_b[d]))
        ffn1_w = (row(ffn1_norm[d]), ffn1_w_gate[d].astype(BF16), ffn1_w_up[d].astype(BF16),
                  ffn1_w_down[d].astype(BF16))
        ffn2_w = (row(ffn2_norm[d]), ffn2_w_gate[d].astype(BF16), ffn2_w_up[d].astype(BF16),
                  ffn2_w_down[d].astype(BF16))
        wo_sb = w_out[d][:SB_WIDTH].astype(BF16)
        wo_rw = w_out[d][SB_WIDTH:].astype(BF16)
        gf = row(final_norm)

        h = _ffn1(h, *ffn1_w)
        h_meta = _ffn1(h_meta, *ffn1_w)
        qkv, rkv, lora = _in_proj(h, row(mix_norm[d]), w_qkv, w_rkv, w_lora)
        qkv_m, rkv_m, lora_m = _in_proj(h_meta, row(mix_norm[d]), w_qkv, w_rkv, w_lora)

        sb = _sb_attention(qkv.reshape(b, l, -1), qkv_m)

        zero_state = jnp.zeros((N_PAIRS, PAIR, PAIR), F32)
        _, h_meta_state = _rwkv(rkv_m[None], lora_m[None], jnp.zeros((1, n_rkv), F32),
                                jnp.zeros((1, LORA_PAD), F32), zero_state, rwkv_params)
        rw, _ = _rwkv(rkv.reshape(b, l, -1), lora.reshape(b, l, -1), rkv_m[-1:], lora_m[-1:],
                      h_meta_state[0], rwkv_params)

        h = _ffn2(h, sb.reshape(b * l, -1), rw.reshape(b * l, -1), wo_sb, wo_rw, *ffn2_w, gf)
    return h.reshape(b, l, D_MODEL)
```

```python
import functools

import jax
import jax.numpy as jnp
from jax import lax
from jax.experimental import pallas as pl
from jax.experimental.pallas import tpu as pltpu

D_MODEL = 1024
N_META = 16
HEAD_DIM = 64
SB_WIDTH = 512
RWKV_WIDTH = 512
D_FF = 2816
W_LORA = 32
A_LORA = 32
G_LORA = 96
RMS_EPS = 1e-6
LNX_EPS = 64e-5

LANES = 128
META_ROWS = 128
LORA_PAD = 256
CHUNK = 64
PAIR = 2 * HEAD_DIM
N_PAIRS = RWKV_WIDTH // PAIR
VMEM_LIMIT = 56 * 1024 * 1024
SB_Q_TILE = 1024
SB_K_TILE = 256
SB_GROUP = 2

F32 = jnp.float32
BF16 = jnp.bfloat16
HI = lax.Precision.HIGHEST
LOG2_E = 1.4426950408889634
EXP2_CLAMP = 126.0
DEAD_LOG2 = -150.0


def _const_spec(shape):
    zeros = (0,) * len(shape)
    return pl.BlockSpec(shape, lambda *_: zeros, pipeline_mode=pl.Buffered(1))


def _rms(x, g):
    ms = jnp.mean(x * x, axis=-1, keepdims=True)
    return x * lax.rsqrt(ms + RMS_EPS) * g


def _dot(a, b, precision=None):
    return jnp.dot(a, b, preferred_element_type=F32, precision=precision)


def _dot_nt(a, b, precision=None):
    return lax.dot_general(a, b, (((1,), (1,)), ((), ())), preferred_element_type=F32,
                           precision=precision)


def _dot_tn(a, b, precision=None):
    return lax.dot_general(a, b, (((0,), (0,)), ((), ())), preferred_element_type=F32,
                           precision=precision)


def _swiglu_half_step(h, g, wg_ref, wu_ref, wd_ref):
    n = _rms(h, g).astype(BF16)
    gate = _dot(n, wg_ref[...])
    up = _dot(n, wu_ref[...])
    act = (gate * jax.nn.sigmoid(gate) * up).astype(BF16)
    return h + 0.5 * _dot(act, wd_ref[...])


def _ffn1_kernel(x_ref, g_ref, wg_ref, wu_ref, wd_ref, o_ref):
    o_ref[...] = _swiglu_half_step(x_ref[...], g_ref[...], wg_ref, wu_ref, wd_ref)


def _ffn2_kernel(h_ref, sb_ref, rw_ref, wo_sb_ref, wo_rw_ref, g_ref, wg_ref, wu_ref, wd_ref,
                 gf_ref, o_ref):
    h = h_ref[...] + _dot(sb_ref[...], wo_sb_ref[...]) + _dot(rw_ref[...], wo_rw_ref[...])
    h = _swiglu_half_step(h, g_ref[...], wg_ref, wu_ref, wd_ref)
    o_ref[...] = _rms(h, gf_ref[...])


def _in_proj_kernel(h_ref, g_ref, wqkv_ref, wrkv_ref, wlora_ref, qkv_ref, rkv_ref, lora_ref):
    n = _rms(h_ref[...], g_ref[...]).astype(BF16)
    qkv_ref[...] = _dot(n, wqkv_ref[...]).astype(BF16)
    rkv_ref[...] = _dot(n, wrkv_ref[...])
    lora_ref[...] = _dot(n, wlora_ref[...])


def _token_tile(n_tokens, want):
    t = min(want, n_tokens)
    assert n_tokens % t == 0 and t % 8 == 0, (n_tokens, t)
    return t


def _row_spec(t, width):
    return pl.BlockSpec((t, width), lambda i: (i, 0))


def _ffn1(x2d, g, wg, wu, wd):
    n = x2d.shape[0]
    t = _token_tile(n, 512)
    return pl.pallas_call(
        _ffn1_kernel,
        grid=(n // t,),
        in_specs=[_row_spec(t, D_MODEL), _const_spec((1, D_MODEL)), _const_spec(wg.shape),
                  _const_spec(wu.shape), _const_spec(wd.shape)],
        out_specs=_row_spec(t, D_MODEL),
        out_shape=jax.ShapeDtypeStruct((n, D_MODEL), F32),
        compiler_params=pltpu.CompilerParams(dimension_semantics=("parallel",),
                                             vmem_limit_bytes=VMEM_LIMIT),
        name="ffn1",
    )(x2d, g, wg, wu, wd)


def _ffn2(h2d, sb2d, rw2d, wo_sb, wo_rw, g, wg, wu, wd, gf):
    n = h2d.shape[0]
    t = _token_tile(n, 512)
    return pl.pallas_call(
        _ffn2_kernel,
        grid=(n // t,),
        in_specs=[_row_spec(t, D_MODEL), _row_spec(t, SB_WIDTH), _row_spec(t, RWKV_WIDTH),
                  _const_spec(wo_sb.shape), _const_spec(wo_rw.shape), _const_spec((1, D_MODEL)),
                  _const_spec(wg.shape), _const_spec(wu.shape), _const_spec(wd.shape),
                  _const_spec((1, D_MODEL))],
        out_specs=_row_spec(t, D_MODEL),
        out_shape=jax.ShapeDtypeStruct((n, D_MODEL), F32),
        compiler_params=pltpu.CompilerParams(dimension_semantics=("parallel",),
                                             vmem_limit_bytes=VMEM_LIMIT),
        name="ffn2",
    )(h2d, sb2d, rw2d, wo_sb, wo_rw, g, wg, wu, wd, gf)


def _in_proj(h2d, g, wqkv, wrkv, wlora):
    n = h2d.shape[0]
    t = _token_tile(n, 512)
    return pl.pallas_call(
        _in_proj_kernel,
        grid=(n // t,),
        in_specs=[_row_spec(t, D_MODEL), _const_spec((1, D_MODEL)), _const_spec(wqkv.shape),
                  _const_spec(wrkv.shape), _const_spec(wlora.shape)],
        out_specs=[_row_spec(t, 3 * SB_WIDTH), _row_spec(t, 3 * RWKV_WIDTH),
                   _row_spec(t, LORA_PAD)],
        out_shape=[jax.ShapeDtypeStruct((n, 3 * SB_WIDTH), BF16),
                   jax.ShapeDtypeStruct((n, 3 * RWKV_WIDTH), F32),
                   jax.ShapeDtypeStruct((n, LORA_PAD), F32)],
        compiler_params=pltpu.CompilerParams(dimension_semantics=("parallel",),
                                             vmem_limit_bytes=VMEM_LIMIT),
        name="in_proj",
    )(h2d, g, wqkv, wrkv, wlora)


def _sb_attn_kernel(q_ref, k_ref, v_ref, km_ref, vm_ref, o_ref, qh_sc, acc_sc, rest_sc, *, tq, tk):
    i = pl.program_id(2)
    q = q_ref[0]
    low = lax.broadcasted_iota(jnp.int32, (1, LANES), 1) < HEAD_DIM
    zero = jnp.zeros_like(q)
    qh_sc[0] = jnp.where(low, q, zero)
    qh_sc[1] = jnp.where(low, zero, q)
    acc_sc[...] = jnp.zeros_like(acc_sc)
    rest_sc[...] = jnp.zeros_like(rest_sc)

    row = lax.broadcasted_iota(jnp.int32, (tk, tk), 0)
    col = lax.broadcasted_iota(jnp.int32, (tk, tk), 1)
    suffix_w = jnp.where(row > col, -1.0, 0.0).astype(BF16)
    diag_mask = col < row
    meta_mask = lax.broadcasted_iota(jnp.int32, (1, META_ROWS), 1) >= META_ROWS - N_META

    def sweep(chains, w, mask):
        z = [_dot_nt(qh_sc[hh, rows, :], kb) for hh, rows, kb, _, _ in chains]
        sp = [jnp.maximum(x, jnp.log2(1.0 + jnp.exp2(jnp.minimum(x, EXP2_CLAMP)))) for x in z]
        if mask is not None:
            sp = [jnp.where(mask, x, 0.0) for x in sp]
        cum = [_dot(x.astype(BF16), w) for x in sp]
        p = [jnp.exp2(a - b + c) for a, b, c in zip(z, sp, cum)]
        if mask is not None:
            p = [jnp.where(mask, x, 0.0) for x in p]
        pv = [_dot(x.astype(BF16), c[3]) for x, c in zip(p, chains)]
        for (hh, rows, _, _, valid), s, c, o in zip(chains, sp, cum, pv):
            rest = rest_sc[hh, rows, :]
            scale = jnp.exp2(rest)
            total = c[:, :1] - s[:, :1]
            if valid is not None:
                scale = scale * valid
                total = total * valid
            acc_sc[hh, rows, :] += scale * o
            rest_sc[hh, rows, :] = rest + total

    n_blk = tq // tk

    def wave(s, mask):
        for rb0 in range(0, n_blk, SB_GROUP):
            chains = []
            for rb in range(rb0, rb0 + SB_GROUP):
                j = i * n_blk + rb - s
                valid = None if mask is not None else (j >= 0).astype(F32)
                start = pl.multiple_of(jnp.maximum(j, 0) * tk, tk)
                kb = k_ref[0, pl.ds(start, tk), :]
                vb = v_ref[0, pl.ds(start, tk), :]
                chains += [(hh, slice(rb * tk, (rb + 1) * tk), kb, vb, valid) for hh in range(2)]
            sweep(chains, suffix_w, mask)

    def alive():
        return jnp.maximum(jnp.max(rest_sc[0]), jnp.max(rest_sc[1])) > DEAD_LOG2

    wave(0, diag_mask)
    last = i * n_blk + n_blk - 1
    lax.while_loop(lambda c: (c[0] <= last) & c[1],
                   lambda c: (wave(c[0], None), (c[0] + 1, alive()))[1],
                   (jnp.int32(1), alive()))

    @pl.when(alive())
    def _():
        sweep([(hh, slice(0, tq), km_ref[...], vm_ref[...], None) for hh in range(2)],
              suffix_w[tk - META_ROWS:, tk - META_ROWS:], meta_mask)

    o_ref[0] = jnp.where(low, acc_sc[0], acc_sc[1]).astype(o_ref.dtype)


def _sb_attention(qkv, qkv_meta):
    b, l, _ = qkv.shape
    tq = min(SB_Q_TILE, l)
    tk = min(SB_K_TILE, l)
    assert l % tq == 0 and tq % tk == 0 and tk >= META_ROWS
    n_pairs = SB_WIDTH // LANES
    return pl.pallas_call(
        functools.partial(_sb_attn_kernel, tq=tq, tk=tk),
        grid=(b, n_pairs, l // tq),
        in_specs=[
            pl.BlockSpec((1, tq, LANES), lambda bi, hp, i: (bi, i, hp)),
            pl.BlockSpec((1, l, LANES), lambda bi, hp, i: (bi, 0, n_pairs + hp)),
            pl.BlockSpec((1, l, LANES), lambda bi, hp, i: (bi, 0, 2 * n_pairs + hp)),
            pl.BlockSpec((META_ROWS, LANES), lambda bi, hp, i: (0, n_pairs + hp)),
            pl.BlockSpec((META_ROWS, LANES), lambda bi, hp, i: (0, 2 * n_pairs + hp)),
        ],
        out_specs=pl.BlockSpec((1, tq, LANES), lambda bi, hp, i: (bi, i, hp)),
        out_shape=jax.ShapeDtypeStruct((b, l, SB_WIDTH), BF16),
        scratch_shapes=[pltpu.VMEM((2, tq, LANES), BF16), pltpu.VMEM((2, tq, LANES), F32),
                        pltpu.VMEM((2, tq, 1), F32)],
        compiler_params=pltpu.CompilerParams(
            dimension_semantics=("parallel", "parallel", "arbitrary"),
            vmem_limit_bytes=VMEM_LIMIT),
        name="sb_attention",
    )(qkv, qkv, qkv, qkv_meta, qkv_meta)


def _bdot(a, b):
    return _dot(a.astype(BF16), b.astype(BF16))


def _rwkv_kernel(rkv_ref, lora_ref, prev_rkv_ref, prev_lora_ref, h0_ref, mu_rkv_ref, mu_lora_ref,
                 w0_ref, wup_ref, a0_ref, aup_ref, gup_ref, kk_ref, ka_ref, rk_ref, lnw_ref,
                 lnb_ref, y_ref, hT_ref, h_sc, prev_rkv_sc, prev_lora_sc, r_sc, k_sc, v_sc,
                 kk_sc, a_sc, lw_sc, y_sc, *, tile, chunks_per_trip):
    c_w = RWKV_WIDTH

    @pl.when(pl.program_id(1) == 0)
    def _():
        h_sc[...] = h0_ref[...]
        prev_rkv_sc[...] = prev_rkv_ref[...]
        prev_lora_sc[...] = prev_lora_ref[...]

    first_row = lax.broadcasted_iota(jnp.int32, (tile, 1), 0) == 0

    def token_shift(p, prev_sc, mu):
        p_prev = jnp.where(first_row, prev_sc[...], pltpu.roll(p, 1, axis=0))
        prev_sc[...] = p[tile - 1:tile, :]
        return p + (p_prev - p) * mu

    p = token_shift(rkv_ref[0], prev_rkv_sc, mu_rkv_ref[...])
    lora = token_shift(lora_ref[0], prev_lora_sc, mu_lora_ref[...])
    r = p[:, :c_w]
    k = p[:, c_w:2 * c_w]
    v = p[:, 2 * c_w:]
    w = -jax.nn.softplus(-(w0_ref[...] + _bdot(jnp.tanh(lora), wup_ref[...]))) - 0.5
    a = jax.nn.sigmoid(a0_ref[...] + _bdot(lora, aup_ref[...]))
    gate = _bdot(jax.nn.sigmoid(lora), gup_ref[...])

    hr = lax.broadcasted_iota(jnp.int32, (c_w, c_w), 0) // HEAD_DIM
    hc = lax.broadcasted_iota(jnp.int32, (c_w, c_w), 1) // HEAD_DIM
    head_sum = jnp.where(hr == hc, 1.0, 0.0).astype(BF16)

    kk = k * kk_ref[...]
    kk = kk * lax.rsqrt(jnp.maximum(_bdot(kk * kk, head_sum), 1e-24))
    k = k * (1.0 + (a - 1.0) * ka_ref[...])
    r_sc[...] = r
    k_sc[...] = k
    v_sc[...] = v
    kk_sc[...] = kk
    a_sc[...] = a
    lw_sc[...] = -jnp.exp(w)

    rr = lax.broadcasted_iota(jnp.int32, (PAIR, PAIR), 0)
    cc = lax.broadcasted_iota(jnp.int32, (PAIR, PAIR), 1)
    same_head = (rr < HEAD_DIM) == (cc < HEAD_DIM)
    strict_lower = same_head & (cc < rr)
    lower = same_head & (cc <= rr)
    eye_mask = rr == cc
    eye = jnp.where(eye_mask, 1.0, 0.0)
    tr = lax.broadcasted_iota(jnp.int32, (CHUNK, CHUNK), 0)
    tc = lax.broadcasted_iota(jnp.int32, (CHUNK, CHUNK), 1)
    cumsum_w = jnp.where(tc <= tr, 1.0, 0.0).astype(BF16)

    zero = jnp.zeros((PAIR, PAIR), BF16)

    def chunk_operands(c):
        rows = pl.ds(pl.multiple_of(c * CHUNK, CHUNK), CHUNK)
        lw = lw_sc[rows, :]
        lw_hi = lw.astype(BF16)
        g = _dot(cumsum_w, lw_hi) + _dot(cumsum_w, (lw - lw_hi.astype(F32)).astype(BF16))
        g_end = g[CHUNK - 1:CHUNK, :]
        decay_out = jnp.exp(-g)
        decay_end = jnp.exp(g_end - g)
        kk_c = kk_sc[rows, :]
        kka = kk_c * a_sc[rows, :]
        k_c = k_sc[rows, :]
        full = dict(
            a_bar=(-kk_c * jnp.exp(g - lw)).astype(BF16),
            r_bar=(r_sc[rows, :] * jnp.exp(g)).astype(BF16),
            b_til=(kka * decay_out).astype(BF16), k_til=(k_c * decay_out).astype(BF16),
            b_hat=(kka * decay_end).astype(BF16), k_hat=(k_c * decay_end).astype(BF16),
            v=v_sc[rows, :].astype(BF16))
        gamma_end = jnp.exp(g_end)
        masked = ("a_bar", "r_bar", "v", "b_hat", "k_hat")
        out = []
        for pr in range(N_PAIRS):
            sl = slice(PAIR * pr, PAIR * (pr + 1))
            ops = {}
            for name, x in full.items():
                x2 = jnp.concatenate([x[:, sl], x[:, sl]], axis=0)
                ops[name] = jnp.where(same_head, x2, zero) if name in masked else x2
            ops.update(rows=rows, sl=sl, pr=pr, gamma_end=gamma_end[:, sl])
            out.append(ops)
        return out

    def trip(t, _):
        chains = [ops for step in range(chunks_per_trip)
                  for ops in chunk_operands(t * chunks_per_trip + step)]
        s1 = [_dot_nt(jnp.concatenate([o["a_bar"], o["r_bar"]], axis=0),
                      jnp.concatenate([o["b_til"], o["k_til"]], axis=0)) for o in chains]
        n_mat = [jnp.where(strict_lower, s[:PAIR, :PAIR], 0.0) for s in s1]
        a_ak = [jnp.where(strict_lower, s[:PAIR, PAIR:], 0.0).astype(BF16) for s in s1]
        a_rb = [jnp.where(lower, s[PAIR:, :PAIR], 0.0).astype(BF16) for s in s1]
        a_rk = [jnp.where(lower, s[PAIR:, PAIR:], 0.0).astype(BF16) for s in s1]
        inv = [eye + n for n in n_mat]
        power = [n.astype(BF16) for n in n_mat]
        for _ in range(5):
            power = [_dot(p, p).astype(BF16) for p in power]
            inv = [i + _dot(i.astype(BF16), p) for i, p in zip(inv, power)]
        akv = [_dot(a, o["v"]).astype(BF16) for a, o in zip(a_ak, chains)]
        pq = [_dot(i.astype(BF16), jnp.concatenate([o["a_bar"], x], axis=1)).astype(BF16)
              for i, o, x in zip(inv, chains, akv)]
        pqv = [jnp.concatenate([x, jnp.concatenate([zero, o["v"]], axis=1)], axis=0)
               for x, o in zip(pq, chains)]
        ef = [_dot(jnp.concatenate([b, k], axis=1), x) for b, k, x in zip(a_rb, a_rk, pqv)]
        ef = [x[:CHUNK] + x[CHUNK:] for x in ef]
        mg = [_dot_tn(jnp.concatenate([o["b_hat"], o["k_hat"]], axis=0), x)
              for o, x in zip(chains, pqv)]
        for o, x, m in zip(chains, ef, mg):
            r_bar = o["r_bar"][:CHUNK] + o["r_bar"][CHUNK:]
            e_mat = x[:, :PAIR] + r_bar.astype(F32)
            m_mat = m[:, :PAIR] + jnp.where(eye_mask, o["gamma_end"], 0.0)
            eh = _bdot(jnp.concatenate([e_mat, m_mat], axis=0), h_sc[o["pr"]])
            y_sc[o["rows"], o["sl"]] = eh[:CHUNK] + x[:, PAIR:]
            h_sc[o["pr"]] = eh[CHUNK:] + m[:, PAIR:]
        return 0

    lax.fori_loop(0, tile // (CHUNK * chunks_per_trip), trip, 0)

    y = y_sc[...]
    inv_n = 1.0 / HEAD_DIM
    d = y - _bdot(y, head_sum) * inv_n
    yn = d * lax.rsqrt(_bdot(d * d, head_sum) * inv_n + LNX_EPS)
    yn = yn * lnw_ref[...] + lnb_ref[...]
    bonus = _bdot(r_sc[...] * k_sc[...] * rk_ref[...], head_sum) * v_sc[...]
    y_ref[0] = ((yn + bonus) * gate).astype(y_ref.dtype)
    hT_ref[0] = h_sc[...]


def _rwkv(rkv, lora, prev_rkv, prev_lora, h0, params):
    b, l, _ = rkv.shape
    tile = min(512, l)
    chunks_per_trip = min(8, tile // CHUNK)
    assert l % tile == 0 and tile % (CHUNK * chunks_per_trip) == 0
    seq = lambda w: pl.BlockSpec((1, tile, w), lambda bi, t: (bi, t, 0))
    scr = lambda: pltpu.VMEM((tile, RWKV_WIDTH), F32)
    return pl.pallas_call(
        functools.partial(_rwkv_kernel, tile=tile, chunks_per_trip=chunks_per_trip),
        grid=(b, l // tile),
        in_specs=[seq(3 * RWKV_WIDTH), seq(LORA_PAD), _const_spec(prev_rkv.shape),
                  _const_spec(prev_lora.shape), _const_spec(h0.shape)]
                 + [_const_spec(p.shape) for p in params],
        out_specs=[seq(RWKV_WIDTH),
                   pl.BlockSpec((1, N_PAIRS, PAIR, PAIR), lambda bi, t: (bi, 0, 0, 0))],
        out_shape=[jax.ShapeDtypeStruct((b, l, RWKV_WIDTH), BF16),
                   jax.ShapeDtypeStruct((b, N_PAIRS, PAIR, PAIR), F32)],
        scratch_shapes=[pltpu.VMEM((N_PAIRS, PAIR, PAIR), F32),
                        pltpu.VMEM((1, 3 * RWKV_WIDTH), F32), pltpu.VMEM((1, LORA_PAD), F32),
                        scr(), scr(), scr(), scr(), scr(), scr(), scr()],
        compiler_params=pltpu.CompilerParams(dimension_semantics=("parallel", "arbitrary"),
                                             vmem_limit_bytes=VMEM_LIMIT),
        name="rwkv7",
    )(rkv, lora, prev_rkv, prev_lora, h0, *params)


def _pad_rows(w, start, total):
    return jnp.pad(w, ((start, total - start - w.shape[0]), (0, 0)))


def kernel(x, meta_tokens, ffn1_norm, ffn1_w_gate, ffn1_w_up, ffn1_w_down, mix_norm, w_in, rwkv_mu, rwkv_w0, rwkv_w_up, rwkv_a0, rwkv_a_up, rwkv_g_up, rwkv_k_k, rwkv_k_a, rwkv_r_k, rwkv_lnx_w, rwkv_lnx_b, w_out, ffn2_norm, ffn2_w_gate, ffn2_w_up, ffn2_w_down, final_norm):
    b, l, _ = x.shape
    depth = ffn1_norm.shape[0]
    assert depth == 1, "stacked layers would need the meta rows carried through the mixers"
    row =lambda p: p.reshape(1, -1).astype(F32)
    n_rkv = 3 * RWKV_WIDTH
    n_lora = W_LORA + A_LORA + G_LORA

    h = x.reshape(b * l, D_MODEL)
    h_meta = _pad_rows(meta_tokens.astype(F32), META_ROWS - N_META, META_ROWS)
    for d in range(depth):
        w_qkv = w_in[d][:, :3 * SB_WIDTH]
        w_qkv = w_qkv.at[:, :SB_WIDTH].multiply(HEAD_DIM ** -0.5 * LOG2_E).astype(BF16)
        w_rkv = w_in[d][:, 3 * SB_WIDTH:3 * SB_WIDTH + n_rkv].astype(BF16)
        w_lora = jnp.pad(w_in[d][:, 3 * SB_WIDTH + n_rkv:],
                         ((0, 0), (0, LORA_PAD - n_lora))).astype(BF16)
        mu_rkv = row(rwkv_mu[d][:n_rkv])
        mu_lora = jnp.pad(row(rwkv_mu[d][n_rkv:]), ((0, 0), (0, LORA_PAD - n_lora)))
        rwkv_params = (
            mu_rkv, mu_lora, row(rwkv_w0[d]), _pad_rows(rwkv_w_up[d], 0, LORA_PAD),
            row(rwkv_a0[d]), _pad_rows(rwkv_a_up[d], W_LORA, LORA_PAD),
            _pad_rows(rwkv_g_up[d], W_LORA + A_LORA, LORA_PAD), row(rwkv_k_k[d]),
            row(rwkv_k_a[d]), row(rwkv_r_k[d]), row(rwkv_lnx_w[d]), row(rwkv_lnx_b[d]))
        ffn1_w = (row(ffn1_norm[d]), ffn1_w_gate[d].astype(BF16), ffn1_w_up[d].astype(BF16),
                  ffn1_w_down[d].astype(BF16))
        ffn2_w = (row(ffn2_norm[d]), ffn2_w_gate[d].astype(BF16), ffn2_w_up[d].astype(BF16),
                  ffn2_w_down[d].astype(BF16))
        wo_sb = w_out[d][:SB_WIDTH].astype(BF16)
        wo_rw = w_out[d][SB_WIDTH:].astype(BF16)
        gf = row(final_norm)

        h = _ffn1(h, *ffn1_w)
        h_meta = _ffn1(h_meta, *ffn1_w)
        qkv, rkv, lora = _in_proj(h, row(mix_norm[d]), w_qkv, w_rkv, w_lora)
        qkv_m, rkv_m, lora_m = _in_proj(h_meta, row(mix_norm[d]), w_qkv, w_rkv, w_lora)

        sb = _sb_attention(qkv.reshape(b, l, -1), qkv_m)

        zero_state = jnp.zeros((N_PAIRS, PAIR, PAIR), F32)
        _, h_meta_state = _rwkv(rkv_m[None], lora_m[None], jnp.zeros((1, n_rkv), F32),
                                jnp.zeros((1, LORA_PAD), F32), zero_state, rwkv_params)
        rw, _ = _rwkv(rkv.reshape(b, l, -1), lora.reshape(b, l, -1), rkv_m[-1:], lora_m[-1:],
                      h_meta_state[0], rwkv_params)

        h = _ffn2(h, sb.reshape(b * l, -1), rw.reshape(b * l, -1), wo_sb, wo_rw, *ffn2_w, gf)
    return h.reshape(b, l, D_MODEL)
```

```python
import functools

import jax
import jax.numpy as jnp
from jax import lax
from jax.experimental import pallas as pl
from jax.experimental.pallas import tpu as pltpu

D_MODEL = 1024
N_META = 16
HEAD_DIM = 64
SB_WIDTH = 512
RWKV_WIDTH = 512
D_FF = 2816
W_LORA = 32
A_LORA = 32
G_LORA = 96
RMS_EPS = 1e-6
LNX_EPS = 64e-5

LANES = 128
META_ROWS = 128
LORA_PAD = 256
CHUNK = 64
PAIR = 2 * HEAD_DIM
N_PAIRS = RWKV_WIDTH // PAIR
VMEM_LIMIT = 56 * 1024 * 1024
TOKEN_TILE = 512
RWKV_TILE = 512
SB_Q_TILE = 1024
SB_K_TILE = 256
SB_GROUP = 2

F32 = jnp.float32
BF16 = jnp.bfloat16
LOG2_E = 1.4426950408889634
EXP2_CLAMP = 126.0
DEAD_LOG2 = -150.0


def _const_spec(shape):
    zeros = (0,) * len(shape)
    return pl.BlockSpec(shape, lambda *_: zeros, pipeline_mode=pl.Buffered(1))


def _rms(x, g):
    ms = jnp.mean(x * x, axis=-1, keepdims=True)
    return x * lax.rsqrt(ms + RMS_EPS) * g


def _dot(a, b):
    return jnp.dot(a, b, preferred_element_type=F32)


def _bdot(a, b):
    return _dot(a.astype(BF16), b.astype(BF16))


def _dot_nt(a, b):
    return lax.dot_general(a, b, (((1,), (1,)), ((), ())), preferred_element_type=F32)


def _dot_tn(a, b):
    return lax.dot_general(a, b, (((0,), (0,)), ((), ())), preferred_element_type=F32)


def _swiglu_half_step(h, g, wg_ref, wu_ref, wd_ref):
    n = _rms(h, g).astype(BF16)
    gate = _dot(n, wg_ref[...])
    up = _dot(n, wu_ref[...])
    act = (gate * jax.nn.sigmoid(gate) * up).astype(BF16)
    return h + 0.5 * _dot(act, wd_ref[...])


def _ffn1_kernel(x_ref, g_ref, wg_ref, wu_ref, wd_ref, o_ref):
    o_ref[...] = _swiglu_half_step(x_ref[...], g_ref[...], wg_ref, wu_ref, wd_ref)


def _ffn2_kernel(h_ref, sb_ref, rw_ref, wo_sb_ref, wo_rw_ref, g_ref, wg_ref, wu_ref, wd_ref,
                 gf_ref, o_ref):
    h = h_ref[...] + _dot(sb_ref[...], wo_sb_ref[...]) + _dot(rw_ref[...], wo_rw_ref[...])
    h = _swiglu_half_step(h, g_ref[...], wg_ref, wu_ref, wd_ref)
    o_ref[...] = _rms(h, gf_ref[...])


def _in_proj_kernel(h_ref, g_ref, wqkv_ref, wrkv_ref, wlora_ref, qkv_ref, rkv_ref, lora_ref):
    n = _rms(h_ref[...], g_ref[...]).astype(BF16)
    qkv_ref[...] = _dot(n, wqkv_ref[...]).astype(BF16)
    rkv_ref[...] = _dot(n, wrkv_ref[...])
    lora_ref[...] = _dot(n, wlora_ref[...])


def _token_tile(n_tokens):
    t = min(TOKEN_TILE, n_tokens)
    assert n_tokens % t == 0 and t % 8 == 0, (n_tokens, t)
    return t


def _row_spec(t, width):
    return pl.BlockSpec((t, width), lambda i: (i, 0))


def _token_params():
    return pltpu.CompilerParams(dimension_semantics=("parallel",), vmem_limit_bytes=VMEM_LIMIT)


def _ffn1(x2d, g, wg, wu, wd):
    n = x2d.shape[0]
    t = _token_tile(n)
    return pl.pallas_call(
        _ffn1_kernel,
        grid=(n // t,),
        in_specs=[_row_spec(t, D_MODEL), _const_spec((1, D_MODEL)), _const_spec(wg.shape),
                  _const_spec(wu.shape), _const_spec(wd.shape)],
        out_specs=_row_spec(t, D_MODEL),
        out_shape=jax.ShapeDtypeStruct((n, D_MODEL), F32),
        compiler_params=_token_params(),
        name="ffn1",
    )(x2d, g, wg, wu, wd)


def _ffn2(h2d, sb2d, rw2d, wo_sb, wo_rw, g, wg, wu, wd, gf):
    n = h2d.shape[0]
    t = _token_tile(n)
    return pl.pallas_call(
        _ffn2_kernel,
        grid=(n // t,),
        in_specs=[_row_spec(t, D_MODEL), _row_spec(t, SB_WIDTH), _row_spec(t, RWKV_WIDTH),
                  _const_spec(wo_sb.shape), _const_spec(wo_rw.shape), _const_spec((1, D_MODEL)),
                  _const_spec(wg.shape), _const_spec(wu.shape), _const_spec(wd.shape),
                  _const_spec((1, D_MODEL))],
        out_specs=_row_spec(t, D_MODEL),
        out_shape=jax.ShapeDtypeStruct((n, D_MODEL), F32),
        compiler_params=_token_params(),
        name="ffn2",
    )(h2d, sb2d, rw2d, wo_sb, wo_rw, g, wg, wu, wd, gf)


def _in_proj(h2d, g, wqkv, wrkv, wlora):
    n = h2d.shape[0]
    t = _token_tile(n)
    return pl.pallas_call(
        _in_proj_kernel,
        grid=(n // t,),
        in_specs=[_row_spec(t, D_MODEL), _const_spec((1, D_MODEL)), _const_spec(wqkv.shape),
                  _const_spec(wrkv.shape), _const_spec(wlora.shape)],
        out_specs=[_row_spec(t, 3 * SB_WIDTH), _row_spec(t, 3 * RWKV_WIDTH),
                   _row_spec(t, LORA_PAD)],
        out_shape=[jax.ShapeDtypeStruct((n, 3 * SB_WIDTH), BF16),
                   jax.ShapeDtypeStruct((n, 3 * RWKV_WIDTH), F32),
                   jax.ShapeDtypeStruct((n, LORA_PAD), F32)],
        compiler_params=_token_params(),
        name="in_proj",
    )(h2d, g, wqkv, wrkv, wlora)


def _sb_attn_kernel(q_ref, k_ref, v_ref, km_ref, vm_ref, o_ref, qh_sc, acc_sc, rest_sc, *, tq, tk):
    i = pl.program_id(2)
    q = q_ref[0]
    low = lax.broadcasted_iota(jnp.int32, (1, LANES), 1) < HEAD_DIM
    zero = jnp.zeros_like(q)
    qh_sc[0] = jnp.where(low, q, zero)
    qh_sc[1] = jnp.where(low, zero, q)
    acc_sc[...] = jnp.zeros_like(acc_sc)
    rest_sc[...] = jnp.zeros_like(rest_sc)

    row = lax.broadcasted_iota(jnp.int32, (tk, tk), 0)
    col = lax.broadcasted_iota(jnp.int32, (tk, tk), 1)
    suffix_w = jnp.where(row > col, -1.0, 0.0).astype(BF16)
    diag_mask = col < row
    meta_mask = lax.broadcasted_iota(jnp.int32, (1, META_ROWS), 1) >= META_ROWS - N_META

    def sweep(chains):
        z = [_dot_nt(qh_sc[c[0], c[1], :], c[2]) for c in chains]
        sp = [jnp.maximum(x, jnp.log2(1.0 + jnp.exp2(jnp.minimum(x, EXP2_CLAMP)))) for x in z]
        sp = [x if c[5] is None else jnp.where(c[5], x, 0.0) for x, c in zip(sp, chains)]
        cum = [_dot(x.astype(BF16), c[4]) for x, c in zip(sp, chains)]
        p = [jnp.exp2(a - b + c) for a, b, c in zip(z, sp, cum)]
        p = [x if c[5] is None else jnp.where(c[5], x, 0.0) for x, c in zip(p, chains)]
        pv = [_dot(x.astype(BF16), c[3]) for x, c in zip(p, chains)]
        for (hh, rows, _, _, _, _, valid), s, c, o in zip(chains, sp, cum, pv):
            rest = rest_sc[hh, rows, :]
            scale = jnp.exp2(rest)
            total = c[:, :1] - s[:, :1]
            if valid is not None:
                scale = scale * valid
                total = total * valid
            acc_sc[hh, rows, :] += scale * o
            rest_sc[hh, rows, :] = rest + total

    n_blk = tq // tk

    def wave(s, mask):
        for rb0 in range(0, n_blk, SB_GROUP):
            chains = []
            for rb in range(rb0, rb0 + SB_GROUP):
                j = i * n_blk + rb - s
                valid = None if mask is not None else (j >= 0).astype(F32)
                start = pl.multiple_of(jnp.maximum(j, 0) * tk, tk)
                kb = k_ref[0, pl.ds(start, tk), :]
                vb = v_ref[0, pl.ds(start, tk), :]
                chains += [(hh, slice(rb * tk, (rb + 1) * tk), kb, vb, suffix_w, mask, valid)
                           for hh in range(2)]
            sweep(chains)

    def alive(s):
        most = jnp.float32(-jnp.inf)
        for rb in range(n_blk):
            rows = slice(rb * tk, (rb + 1) * tk)
            left = jnp.maximum(jnp.max(rest_sc[0, rows, :]), jnp.max(rest_sc[1, rows, :]))
            if s is not None:
                left = jnp.where(i * n_blk + rb - s >= 0, left, -jnp.inf)
            most = jnp.maximum(most, left)
        return most > DEAD_LOG2

    wave(0, diag_mask)
    lax.while_loop(lambda c: c[1],
                   lambda c: (wave(c[0], None), (c[0] + 1, alive(c[0] + 1)))[1],
                   (jnp.int32(1), alive(1)))

    @pl.when(alive(None))
    def _():
        sweep([(hh, slice(0, tq), km_ref[...], vm_ref[...],
                suffix_w[tk - META_ROWS:, tk - META_ROWS:], meta_mask, None) for hh in range(2)])

    o_ref[0] = jnp.where(low, acc_sc[0], acc_sc[1]).astype(o_ref.dtype)


def _sb_attention(qkv, qkv_meta):
    b, l, _ = qkv.shape
    tq = min(SB_Q_TILE, l)
    tk = min(SB_K_TILE, l)
    assert l % tq == 0 and tq % (tk * SB_GROUP) == 0 and tk >= META_ROWS
    n_pairs = SB_WIDTH // LANES
    return pl.pallas_call(
        functools.partial(_sb_attn_kernel, tq=tq, tk=tk),
        grid=(b, n_pairs, l // tq),
        in_specs=[
            pl.BlockSpec((1, tq, LANES), lambda bi, hp, i: (bi, i, hp)),
            pl.BlockSpec((1, l, LANES), lambda bi, hp, i: (bi, 0, n_pairs + hp)),
            pl.BlockSpec((1, l, LANES), lambda bi, hp, i: (bi, 0, 2 * n_pairs + hp)),
            pl.BlockSpec((META_ROWS, LANES), lambda bi, hp, i: (0, n_pairs + hp)),
            pl.BlockSpec((META_ROWS, LANES), lambda bi, hp, i: (0, 2 * n_pairs + hp)),
        ],
        out_specs=pl.BlockSpec((1, tq, LANES), lambda bi, hp, i: (bi, i, hp)),
        out_shape=jax.ShapeDtypeStruct((b, l, SB_WIDTH), BF16),
        scratch_shapes=[pltpu.VMEM((2, tq, LANES), BF16), pltpu.VMEM((2, tq, LANES), F32),
                        pltpu.VMEM((2, tq, 1), F32)],
        compiler_params=pltpu.CompilerParams(
            dimension_semantics=("parallel", "parallel", "arbitrary"),
            vmem_limit_bytes=VMEM_LIMIT),
        name="sb_attention",
    )(qkv, qkv, qkv, qkv_meta, qkv_meta)


def _rwkv_kernel(rkv_ref, lora_ref, prev_rkv_ref, prev_lora_ref, h0_ref, mu_rkv_ref, mu_lora_ref,
                 w0_ref, wup_ref, a0_ref, aup_ref, gup_ref, kk_ref, ka_ref, rk_ref, lnw_ref,
                 lnb_ref, y_ref, hT_ref, h_sc, prev_rkv_sc, prev_lora_sc, r_sc, k_sc, v_sc,
                 kk_sc, a_sc, lw_sc, y_sc, *, tile):
    c_w = RWKV_WIDTH

    @pl.when(pl.program_id(1) == 0)
    def _():
        h_sc[...] = h0_ref[...]
        prev_rkv_sc[...] = prev_rkv_ref[...]
        prev_lora_sc[...] = prev_lora_ref[...]

    rr = lax.broadcasted_iota(jnp.int32, (PAIR, PAIR), 0)
    cc = lax.broadcasted_iota(jnp.int32, (PAIR, PAIR), 1)
    same_head = (rr < HEAD_DIM) == (cc < HEAD_DIM)
    pair_sum = jnp.where(same_head, 1.0, 0.0).astype(BF16)

    def head_sums(x):
        x = x.astype(BF16)
        return jnp.concatenate([_dot(x[:, PAIR * pr:PAIR * (pr + 1)], pair_sum)
                                for pr in range(N_PAIRS)], axis=1)

    def prepare(r0, n):
        rows = slice(r0, r0 + n)
        first_row = lax.broadcasted_iota(jnp.int32, (n, 1), 0) == 0

        def token_shift(x_ref, prev_sc, mu):
            x = x_ref[0, rows, :]
            before = prev_sc[...] if r0 == 0 else x_ref[0, r0 - 1:r0, :]
            x_prev = jnp.where(first_row, before, pltpu.roll(x, 1, axis=0))
            return x + (x_prev - x) * mu

        p = token_shift(rkv_ref, prev_rkv_sc, mu_rkv_ref[...])
        lora = token_shift(lora_ref, prev_lora_sc, mu_lora_ref[...])
        r = p[:, :c_w]
        k = p[:, c_w:2 * c_w]
        w = -jax.nn.softplus(-(w0_ref[...] + _bdot(jnp.tanh(lora), wup_ref[...]))) - 0.5
        a = jax.nn.sigmoid(a0_ref[...] + _bdot(lora, aup_ref[...]))
        kk = k * kk_ref[...]
        r_sc[rows, :] = r
        k_sc[rows, :] = k * (1.0 + (a - 1.0) * ka_ref[...])
        v_sc[rows, :] = p[:, 2 * c_w:]
        kk_sc[rows, :] = kk * lax.rsqrt(jnp.maximum(head_sums(kk * kk), 1e-24))
        a_sc[rows, :] = a
        lw_sc[rows, :] = -jnp.exp(w)
        return _bdot(jax.nn.sigmoid(lora), gup_ref[...])

    strict_lower = same_head & (cc < rr)
    lower = same_head & (cc <= rr)
    eye_mask = rr == cc
    eye = jnp.where(eye_mask, 1.0, 0.0)
    tr = lax.broadcasted_iota(jnp.int32, (CHUNK, CHUNK), 0)
    tc = lax.broadcasted_iota(jnp.int32, (CHUNK, CHUNK), 1)
    cumsum_w = jnp.where(tc <= tr, 1.0, 0.0).astype(BF16)
    zero = jnp.zeros((PAIR, PAIR), BF16)

    def chunk_operands(c):
        rows = slice(c * CHUNK, (c + 1) * CHUNK)
        lw = lw_sc[rows, :]
        lw_hi = lw.astype(BF16)
        g = _dot(cumsum_w, lw_hi) + _dot(cumsum_w, (lw - lw_hi.astype(F32)).astype(BF16))
        g_end = g[CHUNK - 1:CHUNK, :]
        decay_out = jnp.exp(-g)
        decay_end = jnp.exp(g_end - g)
        kk_c = kk_sc[rows, :]
        kka = kk_c * a_sc[rows, :]
        k_c = k_sc[rows, :]
        full = dict(
            a_bar=(-kk_c * jnp.exp(g - lw)).astype(BF16),
            r_bar=(r_sc[rows, :] * jnp.exp(g)).astype(BF16),
            b_til=(kka * decay_out).astype(BF16), k_til=(k_c * decay_out).astype(BF16),
            b_hat=(kka * decay_end).astype(BF16), k_hat=(k_c * decay_end).astype(BF16),
            v=v_sc[rows, :].astype(BF16))
        gamma_end = jnp.exp(g_end)
        masked = ("a_bar", "r_bar", "v", "b_hat", "k_hat")
        out = []
        for pr in range(N_PAIRS):
            sl = slice(PAIR * pr, PAIR * (pr + 1))
            ops = {}
            for name, x in full.items():
                x2 = jnp.concatenate([x[:, sl], x[:, sl]], axis=0)
                ops[name] = jnp.where(same_head, x2, zero) if name in masked else x2
            ops.update(rows=rows, sl=sl, pr=pr, gamma_end=gamma_end[:, sl],
                       r_pair=full["r_bar"][:, sl])
            out.append(ops)
        return out

    def state_free(chunks, out):
        chains = [ops for c in chunks for ops in chunk_operands(c)]
        s1 = [_dot_nt(jnp.concatenate([o["a_bar"], o["r_bar"]], axis=0),
                      jnp.concatenate([o["b_til"], o["k_til"]], axis=0)) for o in chains]
        n_mat = [jnp.where(strict_lower, s[:PAIR, :PAIR], 0.0) for s in s1]
        a_ak = [jnp.where(strict_lower, s[:PAIR, PAIR:], 0.0).astype(BF16) for s in s1]
        a_rb = [jnp.where(lower, s[PAIR:, :PAIR], 0.0).astype(BF16) for s in s1]
        a_rk = [jnp.where(lower, s[PAIR:, PAIR:], 0.0).astype(BF16) for s in s1]
        yield
        inv = [eye + n for n in n_mat]
        power = [n.astype(BF16) for n in n_mat]
        for _ in range(5):
            power = [_dot(p, p).astype(BF16) for p in power]
            yield
            inv = [i + _dot(i.astype(BF16), p) for i, p in zip(inv, power)]
            yield
        akv = [_dot(a, o["v"]).astype(BF16) for a, o in zip(a_ak, chains)]
        yield
        pq = [_dot(i.astype(BF16), jnp.concatenate([o["a_bar"], x], axis=1)).astype(BF16)
              for i, o, x in zip(inv, chains, akv)]
        yield
        pqv = [jnp.concatenate([x, jnp.concatenate([zero, o["v"]], axis=1)], axis=0)
               for x, o in zip(pq, chains)]
        ef = [_dot(jnp.concatenate([b, k], axis=1), x) for b, k, x in zip(a_rb, a_rk, pqv)]
        ef = [x[:CHUNK] + x[CHUNK:] for x in ef]
        yield
        mg = [_dot_tn(jnp.concatenate([o["b_hat"], o["k_hat"]], axis=0), x)
              for o, x in zip(chains, pqv)]
        out.extend(zip(chains, ef, mg))

    def advance_state(items):
        for o, x, m in items:
            e_mat = x[:, :PAIR] + o["r_pair"].astype(F32)
            m_mat = m[:, :PAIR] + jnp.where(eye_mask, o["gamma_end"], 0.0)
            eh = _bdot(jnp.concatenate([e_mat, m_mat], axis=0), h_sc[o["pr"]])
            y_sc[o["rows"], o["sl"]] = eh[:CHUNK] + x[:, PAIR:]
            h_sc[o["pr"]] = eh[CHUNK:] + m[:, PAIR:]

    def finish(r0, n, gate):
        rows = slice(r0, r0 + n)
        y = y_sc[rows, :]
        inv_n = 1.0 / HEAD_DIM
        d = y - head_sums(y) * inv_n
        yn = d * lax.rsqrt(head_sums(d * d) * inv_n + LNX_EPS)
        yn = yn * lnw_ref[...] + lnb_ref[...]
        bonus = head_sums(r_sc[rows, :] * k_sc[rows, :] * rk_ref[...]) * v_sc[rows, :]
        y_ref[0, rows, :] = ((yn + bonus) * gate).astype(y_ref.dtype)

    n_chunks = tile // CHUNK
    first = list(range((n_chunks + 1) // 2))
    second = list(range(len(first), n_chunks))
    n_first = len(first) * CHUNK
    gate_a = prepare(0, n_first)
    done_a, done_b = [], []
    stages_a = state_free(first, done_a)
    next(stages_a)
    gate_b = prepare(n_first, tile - n_first) if second else None
    for _ in stages_a:
        pass
    pending = [functools.partial(advance_state, done_a[c * N_PAIRS:(c + 1) * N_PAIRS])
               for c in range(len(first))] + [functools.partial(finish, 0, n_first, gate_a)]
    for _ in state_free(second, done_b) if second else ():
        if pending:
            pending.pop(0)()
    while pending:
        pending.pop(0)()
    for c in range(len(second)):
        advance_state(done_b[c * N_PAIRS:(c + 1) * N_PAIRS])
    if second:
        finish(n_first, tile - n_first, gate_b)
    prev_rkv_sc[...] = rkv_ref[0, tile - 1:tile, :]
    prev_lora_sc[...] = lora_ref[0, tile - 1:tile, :]
    hT_ref[0] = h_sc[...]


def _rwkv(rkv, lora, prev_rkv, prev_lora, h0, params):
    b, l, _ = rkv.shape
    tile = min(RWKV_TILE, l)
    assert l % tile == 0 and tile % CHUNK == 0
    seq = lambda w: pl.BlockSpec((1, tile, w), lambda bi, t: (bi, t, 0))
    scr = lambda: pltpu.VMEM((tile, RWKV_WIDTH), F32)
    return pl.pallas_call(
        functools.partial(_rwkv_kernel, tile=tile),
        grid=(b, l // tile),
        in_specs=[seq(3 * RWKV_WIDTH), seq(LORA_PAD), _const_spec(prev_rkv.shape),
                  _const_spec(prev_lora.shape), _const_spec(h0.shape)]
                 + [_const_spec(p.shape) for p in params],
        out_specs=[seq(RWKV_WIDTH),
                   pl.BlockSpec((1, N_PAIRS, PAIR, PAIR), lambda bi, t: (bi, 0, 0, 0))],
        out_shape=[jax.ShapeDtypeStruct((b, l, RWKV_WIDTH), BF16),
                   jax.ShapeDtypeStruct((b, N_PAIRS, PAIR, PAIR), F32)],
        scratch_shapes=[pltpu.VMEM((N_PAIRS, PAIR, PAIR), F32),
                        pltpu.VMEM((1, 3 * RWKV_WIDTH), F32), pltpu.VMEM((1, LORA_PAD), F32),
                        scr(), scr(), scr(), scr(), scr(), scr(), scr()],
        compiler_params=pltpu.CompilerParams(dimension_semantics=("parallel", "arbitrary"),
                                             vmem_limit_bytes=VMEM_LIMIT),
        name="rwkv7",
    )(rkv, lora, prev_rkv, prev_lora, h0, *params)


def _pad_rows(w, start, total):
    return jnp.pad(w, ((start, total - start - w.shape[0]), (0, 0)))


def kernel(x, meta_tokens, ffn1_norm, ffn1_w_gate, ffn1_w_up, ffn1_w_down, mix_norm, w_in, rwkv_mu, rwkv_w0, rwkv_w_up, rwkv_a0, rwkv_a_up, rwkv_g_up, rwkv_k_k, rwkv_k_a, rwkv_r_k, rwkv_lnx_w, rwkv_lnx_b, w_out, ffn2_norm, ffn2_w_gate, ffn2_w_up, ffn2_w_down, final_norm):
    b, l, _ = x.shape
    depth = ffn1_norm.shape[0]
    assert depth == 1, "stacked layers would need the meta rows carried through the mixers"
    row = lambda p: p.reshape(1, -1).astype(F32)
    n_rkv = 3 * RWKV_WIDTH
    n_lora = W_LORA + A_LORA + G_LORA

    h = x.reshape(b * l, D_MODEL)
    h_meta = _pad_rows(meta_tokens.astype(F32), META_ROWS - N_META, META_ROWS)
    for d in range(depth):
        w_qkv = w_in[d][:, :3 * SB_WIDTH]
        w_qkv = w_qkv.at[:, :SB_WIDTH].multiply(HEAD_DIM ** -0.5 * LOG2_E).astype(BF16)
        w_rkv = w_in[d][:, 3 * SB_WIDTH:3 * SB_WIDTH + n_rkv].astype(BF16)
        w_lora = jnp.pad(w_in[d][:, 3 * SB_WIDTH + n_rkv:],
                         ((0, 0), (0, LORA_PAD - n_lora))).astype(BF16)
        mu_rkv = row(rwkv_mu[d][:n_rkv])
        mu_lora = jnp.pad(row(rwkv_mu[d][n_rkv:]), ((0, 0), (0, LORA_PAD - n_lora)))
        rwkv_params = (
            mu_rkv, mu_lora, row(rwkv_w0[d]), _pad_rows(rwkv_w_up[d], 0, LORA_PAD),
            row(rwkv_a0[d]), _pad_rows(rwkv_a_up[d], W_LORA, LORA_PAD),
            _pad_rows(rwkv_g_up[d], W_LORA + A_LORA, LORA_PAD), row(rwkv_k_k[d]),
            row(rwkv_k_a[d]), row(rwkv_r_k[d]), row(rwkv_lnx_w[d]), row(rwkv_lnx_b[d]))
        ffn1_w = (row(ffn1_norm[d]), ffn1_w_gate[d].astype(BF16), ffn1_w_up[d].astype(BF16),
                  ffn1_w_down[d].astype(BF16))
        ffn2_w = (row(ffn2_norm[d]), ffn2_w_gate[d].astype(BF16), ffn2_w_up[d].astype(BF16),
                  ffn2_w_down[d].astype(BF16))
        wo_sb = w_out[d][:SB_WIDTH].astype(BF16)
        wo_rw = w_out[d][SB_WIDTH:].astype(BF16)
        gf = row(final_norm)

        h = _ffn1(h, *ffn1_w)
        h_meta = _ffn1(h_meta, *ffn1_w)
        qkv, rkv, lora = _in_proj(h, row(mix_norm[d]), w_qkv, w_rkv, w_lora)
        qkv_m, rkv_m, lora_m = _in_proj(h_meta, row(mix_norm[d]), w_qkv, w_rkv, w_lora)

        sb = _sb_attention(qkv.reshape(b, l, -1), qkv_m)

        zero_state = jnp.zeros((N_PAIRS, PAIR, PAIR), F32)
        _, h_meta_state = _rwkv(rkv_m[None], lora_m[None], jnp.zeros((1, n_rkv), F32),
                                jnp.zeros((1, LORA_PAD), F32), zero_state, rwkv_params)
        rw, _ = _rwkv(rkv.reshape(b, l, -1), lora.reshape(b, l, -1), rkv_m[-1:], lora_m[-1:],
                      h_meta_state[0], rwkv_params)

        h = _ffn2(h, sb.reshape(b * l, -1), rw.reshape(b * l, -1), wo_sb, wo_rw, *ffn2_w, gf)
    return h.reshape(b, l, D_MODEL)
```

```python
import functools

import jax
import jax.numpy as jnp
from jax import lax
from jax.experimental import pallas as pl
from jax.experimental.pallas import tpu as pltpu

D_MODEL = 1024
N_META = 16
HEAD_DIM = 64
SB_WIDTH = 512
RWKV_WIDTH = 512
D_FF = 2816
W_LORA = 32
A_LORA = 32
G_LORA = 96
RMS_EPS = 1e-6
LNX_EPS = 64e-5

LANES = 128
META_ROWS = 128
LORA_PAD = 256
CHUNK = 64
PAIR = 2 * HEAD_DIM
N_PAIRS = RWKV_WIDTH // PAIR
VMEM_LIMIT = 56 * 1024 * 1024
TOKEN_TILE = 512
RWKV_TILE = 512
SB_Q_TILE = 1024
SB_K_TILE = 256
SB_GROUP = 2

F32 = jnp.float32
BF16 = jnp.bfloat16
LOG2_E = 1.4426950408889634
EXP2_CLAMP = 126.0
DEAD_LOG2 = -150.0


def _const_spec(shape):
    zeros = (0,) * len(shape)
    return pl.BlockSpec(shape, lambda *_: zeros, pipeline_mode=pl.Buffered(1))


def _rms(x, g):
    ms = jnp.mean(x * x, axis=-1, keepdims=True)
    return x * lax.rsqrt(ms + RMS_EPS) * g


def _dot(a, b):
    return jnp.dot(a, b, preferred_element_type=F32)


def _bdot(a, b):
    return _dot(a.astype(BF16), b.astype(BF16))


def _dot_nt(a, b):
    return lax.dot_general(a, b, (((1,), (1,)), ((), ())), preferred_element_type=F32)


def _dot_tn(a, b):
    return lax.dot_general(a, b, (((0,), (0,)), ((), ())), preferred_element_type=F32)


def _swiglu_half_step(h, g, wg_ref, wu_ref, wd_ref):
    n = _rms(h, g).astype(BF16)
    gate = _dot(n, wg_ref[...])
    up = _dot(n, wu_ref[...])
    act = (gate * jax.nn.sigmoid(gate) * up).astype(BF16)
    return h + 0.5 * _dot(act, wd_ref[...])


def _ffn1_kernel(x_ref, g_ref, wg_ref, wu_ref, wd_ref, o_ref):
    o_ref[...] = _swiglu_half_step(x_ref[...], g_ref[...], wg_ref, wu_ref, wd_ref)


def _ffn2_kernel(h_ref, sb_ref, rw_ref, wo_sb_ref, wo_rw_ref, g_ref, wg_ref, wu_ref, wd_ref,
                 gf_ref, o_ref):
    h = h_ref[...] + _dot(sb_ref[...], wo_sb_ref[...]) + _dot(rw_ref[...], wo_rw_ref[...])
    h = _swiglu_half_step(h, g_ref[...], wg_ref, wu_ref, wd_ref)
    o_ref[...] = _rms(h, gf_ref[...])


def _in_proj_kernel(h_ref, g_ref, wqkv_ref, wrkv_ref, wlora_ref, mu_rkv_ref, mu_lora_ref,
                    first_rkv_ref, first_lora_ref, qkv_ref, rkv_ref, lora_ref, last_rkv_ref,
                    last_lora_ref, prev_rkv_sc, prev_lora_sc, *, tiles_per_seq):
    tile = h_ref.shape[0]

    @pl.when(pl.program_id(0) % tiles_per_seq == 0)
    def _():
        prev_rkv_sc[...] = first_rkv_ref[...]
        prev_lora_sc[...] = first_lora_ref[...]

    n = _rms(h_ref[...], g_ref[...]).astype(BF16)
    qkv_ref[...] = _dot(n, wqkv_ref[...]).astype(BF16)
    first_row = lax.broadcasted_iota(jnp.int32, (tile, 1), 0) == 0

    def shifted(w_ref, mu_ref, prev_sc, out_ref, last_ref):
        x = _dot(n, w_ref[...])
        x_prev = jnp.where(first_row, prev_sc[...], pltpu.roll(x, 1, axis=0))
        out_ref[...] = x + (x_prev - x) * mu_ref[...]
        prev_sc[...] = x[tile - 1:tile, :]
        last_ref[...] = x[tile - 1:tile, :]

    shifted(wrkv_ref, mu_rkv_ref, prev_rkv_sc, rkv_ref, last_rkv_ref)
    shifted(wlora_ref, mu_lora_ref, prev_lora_sc, lora_ref, last_lora_ref)


def _token_tile(n_tokens):
    t = min(TOKEN_TILE, n_tokens)
    assert n_tokens % t == 0 and t % 8 == 0, (n_tokens, t)
    return t


def _row_spec(t, width):
    return pl.BlockSpec((t, width), lambda i: (i, 0))


def _token_params():
    return pltpu.CompilerParams(dimension_semantics=("parallel",), vmem_limit_bytes=VMEM_LIMIT)


def _ffn1(x2d, g, wg, wu, wd):
    n = x2d.shape[0]
    t = _token_tile(n)
    return pl.pallas_call(
        _ffn1_kernel,
        grid=(n // t,),
        in_specs=[_row_spec(t, D_MODEL), _const_spec((1, D_MODEL)), _const_spec(wg.shape),
                  _const_spec(wu.shape), _const_spec(wd.shape)],
        out_specs=_row_spec(t, D_MODEL),
        out_shape=jax.ShapeDtypeStruct((n, D_MODEL), F32),
        compiler_params=_token_params(),
        name="ffn1",
    )(x2d, g, wg, wu, wd)


def _ffn2(h2d, sb2d, rw2d, wo_sb, wo_rw, g, wg, wu, wd, gf):
    n = h2d.shape[0]
    t = _token_tile(n)
    return pl.pallas_call(
        _ffn2_kernel,
        grid=(n // t,),
        in_specs=[_row_spec(t, D_MODEL), _row_spec(t, SB_WIDTH), _row_spec(t, RWKV_WIDTH),
                  _const_spec(wo_sb.shape), _const_spec(wo_rw.shape), _const_spec((1, D_MODEL)),
                  _const_spec(wg.shape), _const_spec(wu.shape), _const_spec(wd.shape),
                  _const_spec((1, D_MODEL))],
        out_specs=_row_spec(t, D_MODEL),
        out_shape=jax.ShapeDtypeStruct((n, D_MODEL), F32),
        compiler_params=_token_params(),
        name="ffn2",
    )(h2d, sb2d, rw2d, wo_sb, wo_rw, g, wg, wu, wd, gf)


def _in_proj(h2d, seq_len, g, wqkv, wrkv, wlora, mu_rkv, mu_lora, first_rkv, first_lora):
    n = h2d.shape[0]
    t = _token_tile(seq_len)
    n_rkv = 3 * RWKV_WIDTH
    last = lambda w: pl.BlockSpec((1, w), lambda i: (0, 0))
    return pl.pallas_call(
        functools.partial(_in_proj_kernel, tiles_per_seq=seq_len // t),
        grid=(n // t,),
        in_specs=[_row_spec(t, D_MODEL), _const_spec((1, D_MODEL)), _const_spec(wqkv.shape),
                  _const_spec(wrkv.shape), _const_spec(wlora.shape), _const_spec(mu_rkv.shape),
                  _const_spec(mu_lora.shape), _const_spec(first_rkv.shape),
                  _const_spec(first_lora.shape)],
        out_specs=[_row_spec(t, 3 * SB_WIDTH), _row_spec(t, n_rkv), _row_spec(t, LORA_PAD),
                   last(n_rkv), last(LORA_PAD)],
        out_shape=[jax.ShapeDtypeStruct((n, 3 * SB_WIDTH), BF16),
                   jax.ShapeDtypeStruct((n, n_rkv), F32),
                   jax.ShapeDtypeStruct((n, LORA_PAD), F32),
                   jax.ShapeDtypeStruct((1, n_rkv), F32),
                   jax.ShapeDtypeStruct((1, LORA_PAD), F32)],
        scratch_shapes=[pltpu.VMEM((1, n_rkv), F32), pltpu.VMEM((1, LORA_PAD), F32)],
        compiler_params=pltpu.CompilerParams(dimension_semantics=("arbitrary",),
                                             vmem_limit_bytes=VMEM_LIMIT),
        name="in_proj",
    )(h2d, g, wqkv, wrkv, wlora, mu_rkv, mu_lora, first_rkv, first_lora)


def _sb_attn_kernel(q_ref, k_ref, v_ref, km_ref, vm_ref, o_ref, qh_sc, acc_sc, rest_sc, *, tq, tk):
    i = pl.program_id(2)
    q = q_ref[0]
    low = lax.broadcasted_iota(jnp.int32, (1, LANES), 1) < HEAD_DIM
    zero = jnp.zeros_like(q)
    qh_sc[0] = jnp.where(low, q, zero)
    qh_sc[1] = jnp.where(low, zero, q)
    acc_sc[...] = jnp.zeros_like(acc_sc)
    rest_sc[...] = jnp.zeros_like(rest_sc)

    row = lax.broadcasted_iota(jnp.int32, (tk, tk), 0)
    col = lax.broadcasted_iota(jnp.int32, (tk, tk), 1)
    suffix_w = jnp.where(row > col, -1.0, 0.0).astype(BF16)
    diag_mask = col < row
    meta_mask = lax.broadcasted_iota(jnp.int32, (1, META_ROWS), 1) >= META_ROWS - N_META

    def sweep(chains):
        z = [_dot_nt(qh_sc[c[0], c[1], :], c[2]) for c in chains]
        sp = [jnp.maximum(x, jnp.log2(1.0 + jnp.exp2(jnp.minimum(x, EXP2_CLAMP)))) for x in z]
        sp = [x if c[5] is None else jnp.where(c[5], x, 0.0) for x, c in zip(sp, chains)]
        cum = [_dot(x.astype(BF16), c[4]) for x, c in zip(sp, chains)]
        p = [jnp.exp2((a - b + c).astype(BF16)) for a, b, c in zip(z, sp, cum)]
        p = [x if c[5] is None else jnp.where(c[5], x, jnp.zeros_like(x))
             for x, c in zip(p, chains)]
        pv = [_dot(x, c[3]) for x, c in zip(p, chains)]
        for (hh, rows, _, _, _, _, valid), s, c, o in zip(chains, sp, cum, pv):
            rest = rest_sc[hh, rows, :]
            scale = jnp.exp2(rest)
            total = c[:, :1] - s[:, :1]
            if valid is not None:
                scale = scale * valid
                total = total * valid
            acc_sc[hh, rows, :] += scale * o
            rest_sc[hh, rows, :] = rest + total

    n_blk = tq // tk

    def wave(s, mask):
        for rb0 in range(0, n_blk, SB_GROUP):
            chains = []
            for rb in range(rb0, rb0 + SB_GROUP):
                j = i * n_blk + rb - s
                valid = None if mask is not None else (j >= 0).astype(F32)
                start = pl.multiple_of(jnp.maximum(j, 0) * tk, tk)
                kb = k_ref[0, pl.ds(start, tk), :]
                vb = v_ref[0, pl.ds(start, tk), :]
                chains += [(hh, slice(rb * tk, (rb + 1) * tk), kb, vb, suffix_w, mask, valid)
                           for hh in range(2)]
            sweep(chains)

    def alive(s):
        most = jnp.float32(-jnp.inf)
        for rb in range(n_blk):
            rows = slice(rb * tk, (rb + 1) * tk)
            left = jnp.maximum(jnp.max(rest_sc[0, rows, :]), jnp.max(rest_sc[1, rows, :]))
            if s is not None:
                left = jnp.where(i * n_blk + rb - s >= 0, left, -jnp.inf)
            most = jnp.maximum(most, left)
        return most > DEAD_LOG2

    wave(0, diag_mask)
    lax.while_loop(lambda c: c[1],
                   lambda c: (wave(c[0], None), (c[0] + 1, alive(c[0] + 1)))[1],
                   (jnp.int32(1), alive(1)))

    @pl.when(alive(None))
    def _():
        sweep([(hh, slice(0, tq), km_ref[...], vm_ref[...],
                suffix_w[tk - META_ROWS:, tk - META_ROWS:], meta_mask, None) for hh in range(2)])

    o_ref[0] = jnp.where(low, acc_sc[0], acc_sc[1]).astype(o_ref.dtype)


def _sb_attention(qkv, qkv_meta):
    b, l, _ = qkv.shape
    tq = min(SB_Q_TILE, l)
    tk = min(SB_K_TILE, l)
    assert l % tq == 0 and tq % (tk * SB_GROUP) == 0 and tk >= META_ROWS
    n_pairs = SB_WIDTH // LANES
    return pl.pallas_call(
        functools.partial(_sb_attn_kernel, tq=tq, tk=tk),
        grid=(b, n_pairs, l // tq),
        in_specs=[
            pl.BlockSpec((1, tq, LANES), lambda bi, hp, i: (bi, i, hp)),
            pl.BlockSpec((1, l, LANES), lambda bi, hp, i: (bi, 0, n_pairs + hp)),
            pl.BlockSpec((1, l, LANES), lambda bi, hp, i: (bi, 0, 2 * n_pairs + hp)),
            pl.BlockSpec((META_ROWS, LANES), lambda bi, hp, i: (0, n_pairs + hp)),
            pl.BlockSpec((META_ROWS, LANES), lambda bi, hp, i: (0, 2 * n_pairs + hp)),
        ],
        out_specs=pl.BlockSpec((1, tq, LANES), lambda bi, hp, i: (bi, i, hp)),
        out_shape=jax.ShapeDtypeStruct((b, l, SB_WIDTH), BF16),
        scratch_shapes=[pltpu.VMEM((2, tq, LANES), BF16), pltpu.VMEM((2, tq, LANES), F32),
                        pltpu.VMEM((2, tq, 1), F32)],
        compiler_params=pltpu.CompilerParams(
            dimension_semantics=("parallel", "parallel", "arbitrary"),
            vmem_limit_bytes=VMEM_LIMIT),
        name="sb_attention",
    )(qkv, qkv, qkv, qkv_meta, qkv_meta)


def _rwkv_kernel(rkv_ref, lora_ref, h0_ref, w0_ref, wup_ref, a0_ref, aup_ref, gup_ref, kk_ref,
                 ka_ref, rk_ref, lnw_ref, lnb_ref, y_ref, hT_ref, h_sc, r_sc, k_sc, v_sc, kk_sc,
                 a_sc, lw_sc, y_sc, *, tile):
    c_w = RWKV_WIDTH

    @pl.when(pl.program_id(1) == 0)
    def _():
        h_sc[...] = h0_ref[...]

    rr = lax.broadcasted_iota(jnp.int32, (PAIR, PAIR), 0)
    cc = lax.broadcasted_iota(jnp.int32, (PAIR, PAIR), 1)
    same_head = (rr < HEAD_DIM) == (cc < HEAD_DIM)
    pair_sum = jnp.where(same_head, 1.0, 0.0).astype(BF16)

    def head_sums(x):
        x = x.astype(BF16)
        return jnp.concatenate([_dot(x[:, PAIR * pr:PAIR * (pr + 1)], pair_sum)
                                for pr in range(N_PAIRS)], axis=1)

    def prepare(r0, n):
        rows = slice(r0, r0 + n)
        p = rkv_ref[0, rows, :]
        lora = lora_ref[0, rows, :]
        r = p[:, :c_w]
        k = p[:, c_w:2 * c_w]
        w = -jax.nn.softplus(-(w0_ref[...] + _bdot(jnp.tanh(lora), wup_ref[...]))) - 0.5
        a = jax.nn.sigmoid(a0_ref[...] + _bdot(lora, aup_ref[...]))
        kk = k * kk_ref[...]
        r_sc[rows, :] = r
        k_sc[rows, :] = k * (1.0 + (a - 1.0) * ka_ref[...])
        v_sc[rows, :] = p[:, 2 * c_w:]
        kk_sc[rows, :] = kk * lax.rsqrt(jnp.maximum(head_sums(kk * kk), 1e-24))
        a_sc[rows, :] = a
        lw_sc[rows, :] = -jnp.exp(w)
        return _bdot(jax.nn.sigmoid(lora), gup_ref[...])

    strict_lower = same_head & (cc < rr)
    lower = same_head & (cc <= rr)
    eye_mask = rr == cc
    eye = jnp.where(eye_mask, 1.0, 0.0)
    tr = lax.broadcasted_iota(jnp.int32, (CHUNK, CHUNK), 0)
    tc = lax.broadcasted_iota(jnp.int32, (CHUNK, CHUNK), 1)
    cumsum_w = jnp.where(tc <= tr, 1.0, 0.0).astype(BF16)
    zero = jnp.zeros((PAIR, PAIR), BF16)

    def chunk_operands(c):
        rows = slice(c * CHUNK, (c + 1) * CHUNK)
        lw = lw_sc[rows, :]
        lw_hi = lw.astype(BF16)
        g = _dot(cumsum_w, lw_hi) + _dot(cumsum_w, (lw - lw_hi.astype(F32)).astype(BF16))
        g_end = g[CHUNK - 1:CHUNK, :]
        decay_out = jnp.exp(-g)
        decay_end = jnp.exp(g_end - g)
        kk_c = kk_sc[rows, :]
        kka = kk_c * a_sc[rows, :]
        k_c = k_sc[rows, :]
        full = dict(
            a_bar=(-kk_c * jnp.exp(g - lw)).astype(BF16),
            r_bar=(r_sc[rows, :] * jnp.exp(g)).astype(BF16),
            b_til=(kka * decay_out).astype(BF16), k_til=(k_c * decay_out).astype(BF16),
            b_hat=(kka * decay_end).astype(BF16), k_hat=(k_c * decay_end).astype(BF16),
            v=v_sc[rows, :].astype(BF16))
        gamma_end = jnp.exp(g_end)
        masked = ("a_bar", "r_bar", "v", "b_hat", "k_hat")
        out = []
        for pr in range(N_PAIRS):
            sl = slice(PAIR * pr, PAIR * (pr + 1))
            ops = {}
            for name, x in full.items():
                x2 = jnp.concatenate([x[:, sl], x[:, sl]], axis=0)
                ops[name] = jnp.where(same_head, x2, zero) if name in masked else x2
            ops.update(rows=rows, sl=sl, pr=pr, gamma_end=gamma_end[:, sl],
                       r_pair=full["r_bar"][:, sl])
            out.append(ops)
        return out

    def state_free(chunks, out):
        chains = [ops for c in chunks for ops in chunk_operands(c)]
        s1 = [_dot_nt(jnp.concatenate([o["a_bar"], o["r_bar"]], axis=0),
                      jnp.concatenate([o["b_til"], o["k_til"]], axis=0)) for o in chains]
        n_mat = [jnp.where(strict_lower, s[:PAIR, :PAIR], 0.0) for s in s1]
        a_ak = [jnp.where(strict_lower, s[:PAIR, PAIR:], 0.0).astype(BF16) for s in s1]
        a_rb = [jnp.where(lower, s[PAIR:, :PAIR], 0.0).astype(BF16) for s in s1]
        a_rk = [jnp.where(lower, s[PAIR:, PAIR:], 0.0).astype(BF16) for s in s1]
        yield
        inv = [eye + n for n in n_mat]
        power = [n.astype(BF16) for n in n_mat]
        for _ in range(5):
            power = [_dot(p, p).astype(BF16) for p in power]
            yield
            inv = [i + _dot(i.astype(BF16), p) for i, p in zip(inv, power)]
            yield
        akv = [_dot(a, o["v"]).astype(BF16) for a, o in zip(a_ak, chains)]
        yield
        pq = [_dot(i.astype(BF16), jnp.concatenate([o["a_bar"], x], axis=1)).astype(BF16)
              for i, o, x in zip(inv, chains, akv)]
        yield
        pqv = [jnp.concatenate([x, jnp.concatenate([zero, o["v"]], axis=1)], axis=0)
               for x, o in zip(pq, chains)]
        ef = [_dot(jnp.concatenate([b, k], axis=1), x) for b, k, x in zip(a_rb, a_rk, pqv)]
        ef = [x[:CHUNK] + x[CHUNK:] for x in ef]
        yield
        mg = [_dot_tn(jnp.concatenate([o["b_hat"], o["k_hat"]], axis=0), x)
              for o, x in zip(chains, pqv)]
        out.extend(zip(chains, ef, mg))

    def advance_state(items):
        for o, x, m in items:
            e_mat = x[:, :PAIR] + o["r_pair"].astype(F32)
            m_mat = m[:, :PAIR] + jnp.where(eye_mask, o["gamma_end"], 0.0)
            eh = _bdot(jnp.concatenate([e_mat, m_mat], axis=0), h_sc[o["pr"]])
            y_sc[o["rows"], o["sl"]] = eh[:CHUNK] + x[:, PAIR:]
            h_sc[o["pr"]] = eh[CHUNK:] + m[:, PAIR:]

    def finish(r0, n, gate):
        rows = slice(r0, r0 + n)
        y = y_sc[rows, :]
        inv_n = 1.0 / HEAD_DIM
        d = y - head_sums(y) * inv_n
        yn = d * lax.rsqrt(head_sums(d * d) * inv_n + LNX_EPS)
        yn = yn * lnw_ref[...] + lnb_ref[...]
        bonus = head_sums(r_sc[rows, :] * k_sc[rows, :] * rk_ref[...]) * v_sc[rows, :]
        y_ref[0, rows, :] = ((yn + bonus) * gate).astype(y_ref.dtype)

    n_chunks = tile // CHUNK
    first = list(range((n_chunks + 1) // 2))
    second = list(range(len(first), n_chunks))
    n_first = len(first) * CHUNK
    gate_a = prepare(0, n_first)
    done_a, done_b = [], []
    stages_a = state_free(first, done_a)
    next(stages_a)
    gate_b = prepare(n_first, tile - n_first) if second else None
    for _ in stages_a:
        pass
    pending = [functools.partial(advance_state, done_a[c * N_PAIRS:(c + 1) * N_PAIRS])
               for c in range(len(first))] + [functools.partial(finish, 0, n_first, gate_a)]
    for _ in state_free(second, done_b) if second else ():
        if pending:
            pending.pop(0)()
    while pending:
        pending.pop(0)()
    for c in range(len(second)):
        advance_state(done_b[c * N_PAIRS:(c + 1) * N_PAIRS])
    if second:
        finish(n_first, tile - n_first, gate_b)
    hT_ref[0] = h_sc[...]


def _rwkv(rkv, lora, h0, params):
    b, l, _ = rkv.shape
    tile = min(RWKV_TILE, l)
    assert l % tile == 0 and tile % CHUNK == 0
    seq = lambda w: pl.BlockSpec((1, tile, w), lambda bi, t: (bi, t, 0))
    scr = lambda: pltpu.VMEM((tile, RWKV_WIDTH), F32)
    return pl.pallas_call(
        functools.partial(_rwkv_kernel, tile=tile),
        grid=(b, l // tile),
        in_specs=[seq(3 * RWKV_WIDTH), seq(LORA_PAD), _const_spec(h0.shape)]
                 + [_const_spec(p.shape) for p in params],
        out_specs=[seq(RWKV_WIDTH),
                   pl.BlockSpec((1, N_PAIRS, PAIR, PAIR), lambda bi, t: (bi, 0, 0, 0))],
        out_shape=[jax.ShapeDtypeStruct((b, l, RWKV_WIDTH), BF16),
                   jax.ShapeDtypeStruct((b, N_PAIRS, PAIR, PAIR), F32)],
        scratch_shapes=[pltpu.VMEM((N_PAIRS, PAIR, PAIR), F32),
                        scr(), scr(), scr(), scr(), scr(), scr(), scr()],
        compiler_params=pltpu.CompilerParams(dimension_semantics=("parallel", "arbitrary"),
                                             vmem_limit_bytes=VMEM_LIMIT),
        name="rwkv7",
    )(rkv, lora, h0, *params)


def _pad_rows(w, start, total):
    return jnp.pad(w, ((start, total - start - w.shape[0]), (0, 0)))


def kernel(x, meta_tokens, ffn1_norm, ffn1_w_gate, ffn1_w_up, ffn1_w_down, mix_norm, w_in, rwkv_mu, rwkv_w0, rwkv_w_up, rwkv_a0, rwkv_a_up, rwkv_g_up, rwkv_k_k, rwkv_k_a, rwkv_r_k, rwkv_lnx_w, rwkv_lnx_b, w_out, ffn2_norm, ffn2_w_gate, ffn2_w_up, ffn2_w_down, final_norm):
    b, l, _ = x.shape
    depth = ffn1_norm.shape[0]
    assert depth == 1, "stacked layers would need the meta rows carried through the mixers"
    row = lambda p: p.reshape(1, -1).astype(F32)
    n_rkv = 3 * RWKV_WIDTH
    n_lora = W_LORA + A_LORA + G_LORA

    h = x.reshape(b * l, D_MODEL)
    h_meta = _pad_rows(meta_tokens.astype(F32), META_ROWS - N_META, META_ROWS)
    for d in range(depth):
        w_qkv = w_in[d][:, :3 * SB_WIDTH]
        w_qkv = w_qkv.at[:, :SB_WIDTH].multiply(HEAD_DIM ** -0.5 * LOG2_E).astype(BF16)
        w_rkv = w_in[d][:, 3 * SB_WIDTH:3 * SB_WIDTH + n_rkv].astype(BF16)
        w_lora = jnp.pad(w_in[d][:, 3 * SB_WIDTH + n_rkv:],
                         ((0, 0), (0, LORA_PAD - n_lora))).astype(BF16)
        mu_rkv = row(rwkv_mu[d][:n_rkv])
        mu_lora = jnp.pad(row(rwkv_mu[d][n_rkv:]), ((0, 0), (0, LORA_PAD - n_lora)))
        rwkv_params = (
            row(rwkv_w0[d]), _pad_rows(rwkv_w_up[d], 0, LORA_PAD),
            row(rwkv_a0[d]), _pad_rows(rwkv_a_up[d], W_LORA, LORA_PAD),
            _pad_rows(rwkv_g_up[d], W_LORA + A_LORA, LORA_PAD), row(rwkv_k_k[d]),
            row(rwkv_k_a[d]), row(rwkv_r_k[d]), row(rwkv_lnx_w[d]), row(rwkv_lnx_b[d]))
        ffn1_w = (row(ffn1_norm[d]), ffn1_w_gate[d].astype(BF16), ffn1_w_up[d].astype(BF16),
                  ffn1_w_down[d].astype(BF16))
        ffn2_w = (row(ffn2_norm[d]), ffn2_w_gate[d].astype(BF16), ffn2_w_up[d].astype(BF16),
                  ffn2_w_down[d].astype(BF16))
        wo_sb = w_out[d][:SB_WIDTH].astype(BF16)
        wo_rw = w_out[d][SB_WIDTH:].astype(BF16)
        gf = row(final_norm)

        h = _ffn1(h, *ffn1_w)
        h_meta = _ffn1(h_meta, *ffn1_w)
        in_w = (row(mix_norm[d]), w_qkv, w_rkv, w_lora, mu_rkv, mu_lora)
        qkv_m, rkv_m, lora_m, last_rkv_m, last_lora_m = _in_proj(
            h_meta, META_ROWS, *in_w, jnp.zeros((1, n_rkv), F32), jnp.zeros((1, LORA_PAD), F32))
        qkv, rkv, lora, _, _ = _in_proj(h, l, *in_w, last_rkv_m, last_lora_m)

        sb = _sb_attention(qkv.reshape(b, l, -1), qkv_m)

        zero_state = jnp.zeros((N_PAIRS, PAIR, PAIR), F32)
        _, h_meta_state = _rwkv(rkv_m[None], lora_m[None], zero_state, rwkv_params)
        rw, _ = _rwkv(rkv.reshape(b, l, -1), lora.reshape(b, l, -1), h_meta_state[0],
                      rwkv_params)

        h = _ffn2(h, sb.reshape(b * l, -1), rw.reshape(b * l, -1), wo_sb, wo_rw, *ffn2_w, gf)
    return h.reshape(b, l, D_MODEL)
```

```python
import functools

import jax
import jax.numpy as jnp
from jax import lax
from jax.experimental import pallas as pl
from jax.experimental.pallas import tpu as pltpu

D_MODEL = 1024
N_META = 16
HEAD_DIM = 64
SB_WIDTH = 512
RWKV_WIDTH = 512
D_FF = 2816
W_LORA = 32
A_LORA = 32
G_LORA = 96
RMS_EPS = 1e-6
LNX_EPS = 64e-5

LANES = 128
META_ROWS = 128
LORA_PAD = 256
CHUNK = 64
PAIR = 2 * HEAD_DIM
N_PAIRS = RWKV_WIDTH // PAIR
VMEM_LIMIT = 56 * 1024 * 1024
TOKEN_TILE = 512
RWKV_TILE = 512
SB_Q_TILE = 1024
SB_K_TILE = 256
SB_GROUP = 2

F32 = jnp.float32
BF16 = jnp.bfloat16
LOG2_E = 1.4426950408889634
EXP2_CLAMP = 126.0
DEAD_LOG2 = -150.0


def _const_spec(shape):
    zeros = (0,) * len(shape)
    return pl.BlockSpec(shape, lambda *_: zeros, pipeline_mode=pl.Buffered(1))


def _rms(x, g):
    ms = jnp.mean(x * x, axis=-1, keepdims=True)
    return x * lax.rsqrt(ms + RMS_EPS) * g


def _dot(a, b):
    return jnp.dot(a, b, preferred_element_type=F32)


def _bdot(a, b):
    return _dot(a.astype(BF16), b.astype(BF16))


def _dot_nt(a, b):
    return lax.dot_general(a, b, (((1,), (1,)), ((), ())), preferred_element_type=F32)


def _dot_tn(a, b):
    return lax.dot_general(a, b, (((0,), (0,)), ((), ())), preferred_element_type=F32)


def _swiglu_half_step(h, g, wg_ref, wu_ref, wd_ref):
    n = _rms(h, g).astype(BF16)
    gate = _dot(n, wg_ref[...])
    up = _dot(n, wu_ref[...])
    act = (gate * jax.nn.sigmoid(gate) * up).astype(BF16)
    return h + 0.5 * _dot(act, wd_ref[...])


def _ffn1_kernel(x_ref, g_ref, wg_ref, wu_ref, wd_ref, o_ref):
    o_ref[...] = _swiglu_half_step(x_ref[...], g_ref[...], wg_ref, wu_ref, wd_ref)


def _ffn2_kernel(h_ref, sb_ref, rw_ref, wo_sb_ref, wo_rw_ref, g_ref, wg_ref, wu_ref, wd_ref,
                 gf_ref, o_ref):
    h = h_ref[...] + _dot(sb_ref[...], wo_sb_ref[...]) + _dot(rw_ref[...], wo_rw_ref[...])
    h = _swiglu_half_step(h, g_ref[...], wg_ref, wu_ref, wd_ref)
    o_ref[...] = _rms(h, gf_ref[...])


def _in_proj_kernel(h_ref, g_ref, wqkv_ref, wrkv_ref, wlora_ref, mu_rkv_ref, mu_lora_ref,
                    first_rkv_ref, first_lora_ref, qkv_ref, rkv_ref, lora_ref, last_rkv_ref,
                    last_lora_ref, prev_rkv_sc, prev_lora_sc, *, tiles_per_seq):
    tile = h_ref.shape[0]

    @pl.when(pl.program_id(0) % tiles_per_seq == 0)
    def _():
        prev_rkv_sc[...] = first_rkv_ref[...]
        prev_lora_sc[...] = first_lora_ref[...]

    n = _rms(h_ref[...], g_ref[...]).astype(BF16)
    qkv_ref[...] = _dot(n, wqkv_ref[...]).astype(BF16)
    first_row = lax.broadcasted_iota(jnp.int32, (tile, 1), 0) == 0

    def shifted(w_ref, mu_ref, prev_sc, out_ref, last_ref):
        x = _dot(n, w_ref[...])
        x_prev = jnp.where(first_row, prev_sc[...], pltpu.roll(x, 1, axis=0))
        out_ref[...] = x + (x_prev - x) * mu_ref[...]
        prev_sc[...] = x[tile - 1:tile, :]
        last_ref[...] = x[tile - 1:tile, :]

    shifted(wrkv_ref, mu_rkv_ref, prev_rkv_sc, rkv_ref, last_rkv_ref)
    shifted(wlora_ref, mu_lora_ref, prev_lora_sc, lora_ref, last_lora_ref)


def _token_tile(n_tokens):
    t = min(TOKEN_TILE, n_tokens)
    assert n_tokens % t == 0 and t % 8 == 0, (n_tokens, t)
    return t


def _row_spec(t, width):
    return pl.BlockSpec((t, width), lambda i: (i, 0))


def _token_params():
    return pltpu.CompilerParams(dimension_semantics=("parallel",), vmem_limit_bytes=VMEM_LIMIT)


def _ffn1(x2d, g, wg, wu, wd):
    n = x2d.shape[0]
    t = _token_tile(n)
    return pl.pallas_call(
        _ffn1_kernel,
        grid=(n // t,),
        in_specs=[_row_spec(t, D_MODEL), _const_spec((1, D_MODEL)), _const_spec(wg.shape),
                  _const_spec(wu.shape), _const_spec(wd.shape)],
        out_specs=_row_spec(t, D_MODEL),
        out_shape=jax.ShapeDtypeStruct((n, D_MODEL), F32),
        compiler_params=_token_params(),
        name="ffn1",
    )(x2d, g, wg, wu, wd)


def _ffn2(h2d, sb2d, rw2d, wo_sb, wo_rw, g, wg, wu, wd, gf):
    n = h2d.shape[0]
    t = _token_tile(n)
    return pl.pallas_call(
        _ffn2_kernel,
        grid=(n // t,),
        in_specs=[_row_spec(t, D_MODEL), _row_spec(t, SB_WIDTH), _row_spec(t, RWKV_WIDTH),
                  _const_spec(wo_sb.shape), _const_spec(wo_rw.shape), _const_spec((1, D_MODEL)),
                  _const_spec(wg.shape), _const_spec(wu.shape), _const_spec(wd.shape),
                  _const_spec((1, D_MODEL))],
        out_specs=_row_spec(t, D_MODEL),
        out_shape=jax.ShapeDtypeStruct((n, D_MODEL), F32),
        compiler_params=_token_params(),
        name="ffn2",
    )(h2d, sb2d, rw2d, wo_sb, wo_rw, g, wg, wu, wd, gf)


def _in_proj(h2d, seq_len, g, wqkv, wrkv, wlora, mu_rkv, mu_lora, first_rkv, first_lora):
    n = h2d.shape[0]
    t = _token_tile(seq_len)
    n_rkv = 3 * RWKV_WIDTH
    last = lambda w: pl.BlockSpec((1, w), lambda i: (0, 0))
    return pl.pallas_call(
        functools.partial(_in_proj_kernel, tiles_per_seq=seq_len // t),
        grid=(n // t,),
        in_specs=[_row_spec(t, D_MODEL), _const_spec((1, D_MODEL)), _const_spec(wqkv.shape),
                  _const_spec(wrkv.shape), _const_spec(wlora.shape), _const_spec(mu_rkv.shape),
                  _const_spec(mu_lora.shape), _const_spec(first_rkv.shape),
                  _const_spec(first_lora.shape)],
        out_specs=[_row_spec(t, 3 * SB_WIDTH), _row_spec(t, n_rkv), _row_spec(t, LORA_PAD),
                   last(n_rkv), last(LORA_PAD)],
        out_shape=[jax.ShapeDtypeStruct((n, 3 * SB_WIDTH), BF16),
                   jax.ShapeDtypeStruct((n, n_rkv), F32),
                   jax.ShapeDtypeStruct((n, LORA_PAD), F32),
                   jax.ShapeDtypeStruct((1, n_rkv), F32),
                   jax.ShapeDtypeStruct((1, LORA_PAD), F32)],
        scratch_shapes=[pltpu.VMEM((1, n_rkv), F32), pltpu.VMEM((1, LORA_PAD), F32)],
        compiler_params=pltpu.CompilerParams(dimension_semantics=("arbitrary",),
                                             vmem_limit_bytes=VMEM_LIMIT),
        name="in_proj",
    )(h2d, g, wqkv, wrkv, wlora, mu_rkv, mu_lora, first_rkv, first_lora)


def _sb_attn_kernel(q_ref, k_ref, v_ref, km_ref, vm_ref, o_ref, qh_sc, acc_sc, rest_sc, *, tq, tk):
    i = pl.program_id(2)
    q = q_ref[0]
    low = lax.broadcasted_iota(jnp.int32, (1, LANES), 1) < HEAD_DIM
    zero = jnp.zeros_like(q)
    qh_sc[0] = jnp.where(low, q, zero)
    qh_sc[1] = jnp.where(low, zero, q)
    acc_sc[...] = jnp.zeros_like(acc_sc)
    rest_sc[...] = jnp.zeros_like(rest_sc)

    row = lax.broadcasted_iota(jnp.int32, (tk, tk), 0)
    col = lax.broadcasted_iota(jnp.int32, (tk, tk), 1)
    suffix_w = jnp.where(row > col, -1.0, 0.0).astype(BF16)
    diag_mask = col < row
    meta_mask = lax.broadcasted_iota(jnp.int32, (1, META_ROWS), 1) >= META_ROWS - N_META

    def sweep(chains, w, mask):
        if mask is not None:
            mask = jnp.concatenate([mask, mask], axis=0) if mask.shape[0] > 1 else mask
        z = [_dot_nt(jnp.concatenate([qh_sc[0, rows, :], qh_sc[1, rows, :]], axis=0), kb)
             for rows, kb, _, _ in chains]
        sp = [jnp.maximum(x, jnp.log2(1.0 + jnp.exp2(jnp.minimum(x, EXP2_CLAMP)))) for x in z]
        if mask is not None:
            sp = [jnp.where(mask, x, 0.0) for x in sp]
        sizes = [x.shape[0] for x in sp]
        cum = _dot(jnp.concatenate([x.astype(BF16) for x in sp], axis=0), w)
        cum = [cum[sum(sizes[:n]):sum(sizes[:n + 1])] for n in range(len(sp))]
        p = [jnp.exp2((a - b + c).astype(BF16)) for a, b, c in zip(z, sp, cum)]
        if mask is not None:
            p = [jnp.where(mask, x, jnp.zeros_like(x)) for x in p]
        pv = [_dot(x, c[2]) for x, c in zip(p, chains)]
        for (rows, _, _, valid), s, c, o in zip(chains, sp, cum, pv):
            n = o.shape[0] // 2
            for hh in range(2):
                half = slice(hh * n, (hh + 1) * n)
                rest = rest_sc[hh, rows, :]
                scale = jnp.exp2(rest)
                total = c[half, :1] - s[half, :1]
                if valid is not None:
                    scale = scale * valid
                    total = total * valid
                acc_sc[hh, rows, :] += scale * o[half]
                rest_sc[hh, rows, :] = rest + total

    n_blk = tq // tk

    def wave(s, mask):
        for rb0 in range(0, n_blk, SB_GROUP):
            chains = []
            for rb in range(rb0, rb0 + SB_GROUP):
                j = i * n_blk + rb - s
                valid = None if mask is not None else (j >= 0).astype(F32)
                start = pl.multiple_of(jnp.maximum(j, 0) * tk, tk)
                kb = k_ref[0, pl.ds(start, tk), :]
                vb = v_ref[0, pl.ds(start, tk), :]
                chains.append((slice(rb * tk, (rb + 1) * tk), kb, vb, valid))
            sweep(chains, suffix_w, mask)

    def alive(s):
        most = jnp.float32(-jnp.inf)
        for rb in range(n_blk):
            rows = slice(rb * tk, (rb + 1) * tk)
            left = jnp.maximum(jnp.max(rest_sc[0, rows, :]), jnp.max(rest_sc[1, rows, :]))
            if s is not None:
                left = jnp.where(i * n_blk + rb - s >= 0, left, -jnp.inf)
            most = jnp.maximum(most, left)
        return most > DEAD_LOG2

    wave(0, diag_mask)
    lax.while_loop(lambda c: c[1],
                   lambda c: (wave(c[0], None), (c[0] + 1, alive(c[0] + 1)))[1],
                   (jnp.int32(1), alive(1)))

    @pl.when(alive(None))
    def _():
        sweep([(slice(0, tq), km_ref[...], vm_ref[...], None)],
              suffix_w[tk - META_ROWS:, tk - META_ROWS:], meta_mask)

    o_ref[0] = jnp.where(low, acc_sc[0], acc_sc[1]).astype(o_ref.dtype)


def _sb_attention(qkv, qkv_meta):
    b, l, _ = qkv.shape
    tq = min(SB_Q_TILE, l)
    tk = min(SB_K_TILE, l)
    assert l % tq == 0 and tq % (tk * SB_GROUP) == 0 and tk >= META_ROWS
    n_pairs = SB_WIDTH // LANES
    return pl.pallas_call(
        functools.partial(_sb_attn_kernel, tq=tq, tk=tk),
        grid=(b, n_pairs, l // tq),
        in_specs=[
            pl.BlockSpec((1, tq, LANES), lambda bi, hp, i: (bi, i, hp)),
            pl.BlockSpec((1, l, LANES), lambda bi, hp, i: (bi, 0, n_pairs + hp)),
            pl.BlockSpec((1, l, LANES), lambda bi, hp, i: (bi, 0, 2 * n_pairs + hp)),
            pl.BlockSpec((META_ROWS, LANES), lambda bi, hp, i: (0, n_pairs + hp)),
            pl.BlockSpec((META_ROWS, LANES), lambda bi, hp, i: (0, 2 * n_pairs + hp)),
        ],
        out_specs=pl.BlockSpec((1, tq, LANES), lambda bi, hp, i: (bi, i, hp)),
        out_shape=jax.ShapeDtypeStruct((b, l, SB_WIDTH), BF16),
        scratch_shapes=[pltpu.VMEM((2, tq, LANES), BF16), pltpu.VMEM((2, tq, LANES), F32),
                        pltpu.VMEM((2, tq, 1), F32)],
        compiler_params=pltpu.CompilerParams(
            dimension_semantics=("parallel", "parallel", "arbitrary"),
            vmem_limit_bytes=VMEM_LIMIT),
        name="sb_attention",
    )(qkv, qkv, qkv, qkv_meta, qkv_meta)


def _rwkv_kernel(rkv_ref, lora_ref, h0_ref, w0_ref, wup_ref, a0_ref, aup_ref, gup_ref, kk_ref,
                 ka_ref, rk_ref, lnw_ref, lnb_ref, y_ref, hT_ref, h_sc, r_sc, k_sc, v_sc, kk_sc,
                 a_sc, lw_sc, y_sc, *, tile):
    c_w = RWKV_WIDTH

    @pl.when(pl.program_id(1) == 0)
    def _():
        h_sc[...] = h0_ref[...]

    rr = lax.broadcasted_iota(jnp.int32, (PAIR, PAIR), 0)
    cc = lax.broadcasted_iota(jnp.int32, (PAIR, PAIR), 1)
    same_head = (rr < HEAD_DIM) == (cc < HEAD_DIM)
    pair_sum = jnp.where(same_head, 1.0, 0.0).astype(BF16)

    def head_sums(x):
        x = x.astype(BF16)
        return jnp.concatenate([_dot(x[:, PAIR * pr:PAIR * (pr + 1)], pair_sum)
                                for pr in range(N_PAIRS)], axis=1)

    def prepare(r0, n):
        rows = slice(r0, r0 + n)
        p = rkv_ref[0, rows, :]
        lora = lora_ref[0, rows, :]
        r = p[:, :c_w]
        k = p[:, c_w:2 * c_w]
        w = -jax.nn.softplus(-(w0_ref[...] + _bdot(jnp.tanh(lora), wup_ref[...]))) - 0.5
        a = jax.nn.sigmoid(a0_ref[...] + _bdot(lora, aup_ref[...]))
        kk = k * kk_ref[...]
        r_sc[rows, :] = r
        k_sc[rows, :] = k * (1.0 + (a - 1.0) * ka_ref[...])
        v_sc[rows, :] = p[:, 2 * c_w:]
        kk_sc[rows, :] = kk * lax.rsqrt(jnp.maximum(head_sums(kk * kk), 1e-24))
        a_sc[rows, :] = a
        lw_sc[rows, :] = -jnp.exp(w)
        return _bdot(jax.nn.sigmoid(lora), gup_ref[...])

    strict_lower = same_head & (cc < rr)
    lower = same_head & (cc <= rr)
    eye_mask = rr == cc
    eye = jnp.where(eye_mask, 1.0, 0.0)
    tr = lax.broadcasted_iota(jnp.int32, (CHUNK, CHUNK), 0)
    tc = lax.broadcasted_iota(jnp.int32, (CHUNK, CHUNK), 1)
    cumsum_w = jnp.where(tc <= tr, 1.0, 0.0).astype(BF16)
    zero = jnp.zeros((PAIR, PAIR), BF16)

    def chunk_operands(c):
        rows = slice(c * CHUNK, (c + 1) * CHUNK)
        lw = lw_sc[rows, :]
        lw_hi = lw.astype(BF16)
        g = _dot(cumsum_w, lw_hi) + _dot(cumsum_w, (lw - lw_hi.astype(F32)).astype(BF16))
        g_end = g[CHUNK - 1:CHUNK, :]
        decay_out = jnp.exp(-g)
        decay_end = jnp.exp(g_end - g)
        kk_c = kk_sc[rows, :]
        kka = kk_c * a_sc[rows, :]
        k_c = k_sc[rows, :]
        full = dict(
            a_bar=(-kk_c * jnp.exp(g - lw)).astype(BF16),
            r_bar=(r_sc[rows, :] * jnp.exp(g)).astype(BF16),
            b_til=(kka * decay_out).astype(BF16), k_til=(k_c * decay_out).astype(BF16),
            b_hat=(kka * decay_end).astype(BF16), k_hat=(k_c * decay_end).astype(BF16),
            v=v_sc[rows, :].astype(BF16))
        gamma_end = jnp.exp(g_end)
        masked = ("a_bar", "r_bar", "v", "b_hat", "k_hat")
        out = []
        for pr in range(N_PAIRS):
            sl = slice(PAIR * pr, PAIR * (pr + 1))
            ops = {}
            for name, x in full.items():
                x2 = jnp.concatenate([x[:, sl], x[:, sl]], axis=0)
                ops[name] = jnp.where(same_head, x2, zero) if name in masked else x2
            ops.update(rows=rows, sl=sl, pr=pr, gamma_end=gamma_end[:, sl],
                       r_pair=full["r_bar"][:, sl])
            out.append(ops)
        return out

    def state_free(chunks, out):
        chains = [ops for c in chunks for ops in chunk_operands(c)]
        s1 = [_dot_nt(jnp.concatenate([o["a_bar"], o["r_bar"]], axis=0),
                      jnp.concatenate([o["b_til"], o["k_til"]], axis=0)) for o in chains]
        n_mat = [jnp.where(strict_lower, s[:PAIR, :PAIR], 0.0) for s in s1]
        a_ak = [jnp.where(strict_lower, s[:PAIR, PAIR:], 0.0).astype(BF16) for s in s1]
        a_rb = [jnp.where(lower, s[PAIR:, :PAIR], 0.0).astype(BF16) for s in s1]
        a_rk = [jnp.where(lower, s[PAIR:, PAIR:], 0.0).astype(BF16) for s in s1]
        yield
        inv = [eye + n for n in n_mat]
        power = [n.astype(BF16) for n in n_mat]
        for _ in range(5):
            power = [_dot(p, p).astype(BF16) for p in power]
            yield
            inv = [i + _dot(i.astype(BF16), p) for i, p in zip(inv, power)]
            yield
        akv = [_dot(a, o["v"]).astype(BF16) for a, o in zip(a_ak, chains)]
        yield
        pq = [_dot(i.astype(BF16), jnp.concatenate([o["a_bar"], x], axis=1)).astype(BF16)
              for i, o, x in zip(inv, chains, akv)]
        yield
        pqv = [jnp.concatenate([x, jnp.concatenate([zero, o["v"]], axis=1)], axis=0)
               for x, o in zip(pq, chains)]
        ef = [_dot(jnp.concatenate([b, k], axis=1), x) for b, k, x in zip(a_rb, a_rk, pqv)]
        ef = [x[:CHUNK] + x[CHUNK:] for x in ef]
        yield
        mg = [_dot_tn(jnp.concatenate([o["b_hat"], o["k_hat"]], axis=0), x)
              for o, x in zip(chains, pqv)]
        out.extend(zip(chains, ef, mg))

    def advance_state(items):
        for o, x, m in items:
            e_mat = x[:, :PAIR] + o["r_pair"].astype(F32)
            m_mat = m[:, :PAIR] + jnp.where(eye_mask, o["gamma_end"], 0.0)
            eh = _bdot(jnp.concatenate([e_mat, m_mat], axis=0), h_sc[o["pr"]])
            y_sc[o["rows"], o["sl"]] = eh[:CHUNK] + x[:, PAIR:]
            h_sc[o["pr"]] = eh[CHUNK:] + m[:, PAIR:]

    def finish(r0, n, gate):
        rows = slice(r0, r0 + n)
        y = y_sc[rows, :]
        inv_n = 1.0 / HEAD_DIM
        d = y - head_sums(y) * inv_n
        yn = d * lax.rsqrt(head_sums(d * d) * inv_n + LNX_EPS)
        yn = yn * lnw_ref[...] + lnb_ref[...]
        bonus = head_sums(r_sc[rows, :] * k_sc[rows, :] * rk_ref[...]) * v_sc[rows, :]
        y_ref[0, rows, :] = ((yn + bonus) * gate).astype(y_ref.dtype)

    n_chunks = tile // CHUNK
    first = list(range((n_chunks + 1) // 2))
    second = list(range(len(first), n_chunks))
    n_first = len(first) * CHUNK
    gate_a = prepare(0, n_first)
    done_a, done_b = [], []
    stages_a = state_free(first, done_a)
    next(stages_a)
    gate_b = prepare(n_first, tile - n_first) if second else None
    for _ in stages_a:
        pass
    pending = [functools.partial(advance_state, done_a[c * N_PAIRS:(c + 1) * N_PAIRS])
               for c in range(len(first))] + [functools.partial(finish, 0, n_first, gate_a)]
    for _ in state_free(second, done_b) if second else ():
        if pending:
            pending.pop(0)()
    while pending:
        pending.pop(0)()
    for c in range(len(second)):
        advance_state(done_b[c * N_PAIRS:(c + 1) * N_PAIRS])
    if second:
        finish(n_first, tile - n_first, gate_b)
    hT_ref[0] = h_sc[...]


def _rwkv(rkv, lora, h0, params):
    b, l, _ = rkv.shape
    tile = min(RWKV_TILE, l)
    assert l % tile == 0 and tile % CHUNK == 0
    seq = lambda w: pl.BlockSpec((1, tile, w), lambda bi, t: (bi, t, 0))
    scr = lambda: pltpu.VMEM((tile, RWKV_WIDTH), F32)
    return pl.pallas_call(
        functools.partial(_rwkv_kernel, tile=tile),
        grid=(b, l // tile),
        in_specs=[seq(3 * RWKV_WIDTH), seq(LORA_PAD), _const_spec(h0.shape)]
                 + [_const_spec(p.shape) for p in params],
        out_specs=[seq(RWKV_WIDTH),
                   pl.BlockSpec((1, N_PAIRS, PAIR, PAIR), lambda bi, t: (bi, 0, 0, 0))],
        out_shape=[jax.ShapeDtypeStruct((b, l, RWKV_WIDTH), BF16),
                   jax.ShapeDtypeStruct((b, N_PAIRS, PAIR, PAIR), F32)],
        scratch_shapes=[pltpu.VMEM((N_PAIRS, PAIR, PAIR), F32),
                        scr(), scr(), scr(), scr(), scr(), scr(), scr()],
        compiler_params=pltpu.CompilerParams(dimension_semantics=("parallel", "arbitrary"),
                                             vmem_limit_bytes=VMEM_LIMIT),
        name="rwkv7",
    )(rkv, lora, h0, *params)


def _pad_rows(w, start, total):
    return jnp.pad(w, ((start, total - start - w.shape[0]), (0, 0)))


def kernel(x, meta_tokens, ffn1_norm, ffn1_w_gate, ffn1_w_up, ffn1_w_down, mix_norm, w_in, rwkv_mu, rwkv_w0, rwkv_w_up, rwkv_a0, rwkv_a_up, rwkv_g_up, rwkv_k_k, rwkv_k_a, rwkv_r_k, rwkv_lnx_w, rwkv_lnx_b, w_out, ffn2_norm, ffn2_w_gate, ffn2_w_up, ffn2_w_down, final_norm):
    b, l, _ = x.shape
    depth = ffn1_norm.shape[0]
    assert depth == 1, "stacked layers would need the meta rows carried through the mixers"
    row = lambda p: p.reshape(1, -1).astype(F32)
    n_rkv = 3 * RWKV_WIDTH
    n_lora = W_LORA + A_LORA + G_LORA

    h = x.reshape(b * l, D_MODEL)
    h_meta = _pad_rows(meta_tokens.astype(F32), META_ROWS - N_META, META_ROWS)
    for d in range(depth):
        w_qkv = w_in[d][:, :3 * SB_WIDTH]
        w_qkv = w_qkv.at[:, :SB_WIDTH].multiply(HEAD_DIM ** -0.5 * LOG2_E).astype(BF16)
        w_rkv = w_in[d][:, 3 * SB_WIDTH:3 * SB_WIDTH + n_rkv].astype(BF16)
        w_lora = jnp.pad(w_in[d][:, 3 * SB_WIDTH + n_rkv:],
                         ((0, 0), (0, LORA_PAD - n_lora))).astype(BF16)
        mu_rkv = row(rwkv_mu[d][:n_rkv])
        mu_lora = jnp.pad(row(rwkv_mu[d][n_rkv:]), ((0, 0), (0, LORA_PAD - n_lora)))
        rwkv_params = (
            row(rwkv_w0[d]), _pad_rows(rwkv_w_up[d], 0, LORA_PAD),
            row(rwkv_a0[d]), _pad_rows(rwkv_a_up[d], W_LORA, LORA_PAD),
            _pad_rows(rwkv_g_up[d], W_LORA + A_LORA, LORA_PAD), row(rwkv_k_k[d]),
            row(rwkv_k_a[d]), row(rwkv_r_k[d]), row(rwkv_lnx_w[d]), row(rwkv_lnx_b[d]))
        ffn1_w = (row(ffn1_norm[d]), ffn1_w_gate[d].astype(BF16), ffn1_w_up[d].astype(BF16),
                  ffn1_w_down[d].astype(BF16))
        ffn2_w = (row(ffn2_norm[d]), ffn2_w_gate[d].astype(BF16), ffn2_w_up[d].astype(BF16),
                  ffn2_w_down[d].astype(BF16))
        wo_sb = w_out[d][:SB_WIDTH].astype(BF16)
        wo_rw = w_out[d][SB_WIDTH:].astype(BF16)
        gf = row(final_norm)

        h = _ffn1(h, *ffn1_w)
        h_meta = _ffn1(h_meta, *ffn1_w)
        in_w = (row(mix_norm[d]), w_qkv, w_rkv, w_lora, mu_rkv, mu_lora)
        qkv_m, rkv_m, lora_m, last_rkv_m, last_lora_m = _in_proj(
            h_meta, META_ROWS, *in_w, jnp.zeros((1, n_rkv), F32), jnp.zeros((1, LORA_PAD), F32))
        qkv, rkv, lora, _, _ = _in_proj(h, l, *in_w, last_rkv_m, last_lora_m)

        sb = _sb_attention(qkv.reshape(b, l, -1), qkv_m)

        zero_state = jnp.zeros((N_PAIRS, PAIR, PAIR), F32)
        _, h_meta_state = _rwkv(rkv_m[None], lora_m[None], zero_state, rwkv_params)
        rw, _ = _rwkv(rkv.reshape(b, l, -1), lora.reshape(b, l, -1), h_meta_state[0],
                      rwkv_params)

        h = _ffn2(h, sb.reshape(b * l, -1), rw.reshape(b * l, -1), wo_sb, wo_rw, *ffn2_w, gf)
    return h.reshape(b, l, D_MODEL)
```

```python
import functools

import jax
import jax.numpy as jnp
from jax import lax
from jax.experimental import pallas as pl
from jax.experimental.pallas import tpu as pltpu

D_MODEL = 1024
N_META = 16
HEAD_DIM = 64
SB_WIDTH = 512
RWKV_WIDTH = 512
D_FF = 2816
W_LORA = 32
A_LORA = 32
G_LORA = 96
RMS_EPS = 1e-6
LNX_EPS = 64e-5

LANES = 128
META_ROWS = 128
LORA_PAD = 256
CHUNK = 64
PAIR = 2 * HEAD_DIM
N_PAIRS = RWKV_WIDTH // PAIR
VMEM_LIMIT = 56 * 1024 * 1024
TOKEN_TILE = 512
RWKV_TILE = 512
SB_Q_TILE = 2048
SB_K_TILE = 256
SB_GROUP = 2

F32 = jnp.float32
BF16 = jnp.bfloat16
LOG2_E = 1.4426950408889634
EXP2_CLAMP = 126.0
DEAD_LOG2 = -150.0


def _const_spec(shape):
    zeros = (0,) * len(shape)
    return pl.BlockSpec(shape, lambda *_: zeros, pipeline_mode=pl.Buffered(1))


def _rms(x, g):
    ms = jnp.mean(x * x, axis=-1, keepdims=True)
    return x * lax.rsqrt(ms + RMS_EPS) * g


def _dot(a, b):
    return jnp.dot(a, b, preferred_element_type=F32)


def _bdot(a, b):
    return _dot(a.astype(BF16), b.astype(BF16))


def _dot_nt(a, b):
    return lax.dot_general(a, b, (((1,), (1,)), ((), ())), preferred_element_type=F32)


def _dot_tn(a, b):
    return lax.dot_general(a, b, (((0,), (0,)), ((), ())), preferred_element_type=F32)


def _swiglu_half_step(h, g, wg_ref, wu_ref, wd_ref):
    n = _rms(h, g).astype(BF16)
    gate = _dot(n, wg_ref[...])
    up = _dot(n, wu_ref[...])
    act = (gate * jax.nn.sigmoid(gate) * up).astype(BF16)
    return h + 0.5 * _dot(act, wd_ref[...])


def _ffn1_kernel(x_ref, g_ref, wg_ref, wu_ref, wd_ref, o_ref):
    o_ref[...] = _swiglu_half_step(x_ref[...], g_ref[...], wg_ref, wu_ref, wd_ref)


def _ffn2_kernel(h_ref, sb_ref, rw_ref, wo_sb_ref, wo_rw_ref, g_ref, wg_ref, wu_ref, wd_ref,
                 gf_ref, o_ref):
    h = h_ref[...] + _dot(sb_ref[...], wo_sb_ref[...]) + _dot(rw_ref[...], wo_rw_ref[...])
    h = _swiglu_half_step(h, g_ref[...], wg_ref, wu_ref, wd_ref)
    o_ref[...] = _rms(h, gf_ref[...])


def _in_proj_kernel(h_ref, g_ref, wqkv_ref, wrkv_ref, wlora_ref, mu_rkv_ref, mu_lora_ref,
                    first_rkv_ref, first_lora_ref, qkv_ref, rkv_ref, lora_ref, last_rkv_ref,
                    last_lora_ref, prev_rkv_sc, prev_lora_sc, *, tiles_per_seq):
    tile = h_ref.shape[0]

    @pl.when(pl.program_id(0) % tiles_per_seq == 0)
    def _():
        prev_rkv_sc[...] = first_rkv_ref[...]
        prev_lora_sc[...] = first_lora_ref[...]

    n = _rms(h_ref[...], g_ref[...]).astype(BF16)
    qkv_ref[...] = _dot(n, wqkv_ref[...]).astype(BF16)
    first_row = lax.broadcasted_iota(jnp.int32, (tile, 1), 0) == 0

    def shifted(w_ref, mu_ref, prev_sc, out_ref, last_ref):
        x = _dot(n, w_ref[...])
        x_prev = jnp.where(first_row, prev_sc[...], pltpu.roll(x, 1, axis=0))
        out_ref[...] = x + (x_prev - x) * mu_ref[...]
        prev_sc[...] = x[tile - 1:tile, :]
        last_ref[...] = x[tile - 1:tile, :]

    shifted(wrkv_ref, mu_rkv_ref, prev_rkv_sc, rkv_ref, last_rkv_ref)
    shifted(wlora_ref, mu_lora_ref, prev_lora_sc, lora_ref, last_lora_ref)


def _token_tile(n_tokens):
    t = min(TOKEN_TILE, n_tokens)
    assert n_tokens % t == 0 and t % 8 == 0, (n_tokens, t)
    return t


def _row_spec(t, width):
    return pl.BlockSpec((t, width), lambda i: (i, 0))


def _token_params():
    return pltpu.CompilerParams(dimension_semantics=("parallel",), vmem_limit_bytes=VMEM_LIMIT)


def _ffn1(x2d, g, wg, wu, wd):
    n = x2d.shape[0]
    t = _token_tile(n)
    return pl.pallas_call(
        _ffn1_kernel,
        grid=(n // t,),
        in_specs=[_row_spec(t, D_MODEL), _const_spec((1, D_MODEL)), _const_spec(wg.shape),
                  _const_spec(wu.shape), _const_spec(wd.shape)],
        out_specs=_row_spec(t, D_MODEL),
        out_shape=jax.ShapeDtypeStruct((n, D_MODEL), F32),
        compiler_params=_token_params(),
        name="ffn1",
    )(x2d, g, wg, wu, wd)


def _ffn2(h2d, sb2d, rw2d, wo_sb, wo_rw, g, wg, wu, wd, gf):
    n = h2d.shape[0]
    t = _token_tile(n)
    return pl.pallas_call(
        _ffn2_kernel,
        grid=(n // t,),
        in_specs=[_row_spec(t, D_MODEL), _row_spec(t, SB_WIDTH), _row_spec(t, RWKV_WIDTH),
                  _const_spec(wo_sb.shape), _const_spec(wo_rw.shape), _const_spec((1, D_MODEL)),
                  _const_spec(wg.shape), _const_spec(wu.shape), _const_spec(wd.shape),
                  _const_spec((1, D_MODEL))],
        out_specs=_row_spec(t, D_MODEL),
        out_shape=jax.ShapeDtypeStruct((n, D_MODEL), F32),
        compiler_params=_token_params(),
        name="ffn2",
    )(h2d, sb2d, rw2d, wo_sb, wo_rw, g, wg, wu, wd, gf)


def _in_proj(h2d, seq_len, g, wqkv, wrkv, wlora, mu_rkv, mu_lora, first_rkv, first_lora):
    n = h2d.shape[0]
    t = _token_tile(seq_len)
    n_rkv = 3 * RWKV_WIDTH
    last = lambda w: pl.BlockSpec((1, w), lambda i: (0, 0))
    return pl.pallas_call(
        functools.partial(_in_proj_kernel, tiles_per_seq=seq_len // t),
        grid=(n // t,),
        in_specs=[_row_spec(t, D_MODEL), _const_spec((1, D_MODEL)), _const_spec(wqkv.shape),
                  _const_spec(wrkv.shape), _const_spec(wlora.shape), _const_spec(mu_rkv.shape),
                  _const_spec(mu_lora.shape), _const_spec(first_rkv.shape),
                  _const_spec(first_lora.shape)],
        out_specs=[_row_spec(t, 3 * SB_WIDTH), _row_spec(t, n_rkv), _row_spec(t, LORA_PAD),
                   last(n_rkv), last(LORA_PAD)],
        out_shape=[jax.ShapeDtypeStruct((n, 3 * SB_WIDTH), BF16),
                   jax.ShapeDtypeStruct((n, n_rkv), F32),
                   jax.ShapeDtypeStruct((n, LORA_PAD), F32),
                   jax.ShapeDtypeStruct((1, n_rkv), F32),
                   jax.ShapeDtypeStruct((1, LORA_PAD), F32)],
        scratch_shapes=[pltpu.VMEM((1, n_rkv), F32), pltpu.VMEM((1, LORA_PAD), F32)],
        compiler_params=pltpu.CompilerParams(dimension_semantics=("arbitrary",),
                                             vmem_limit_bytes=VMEM_LIMIT),
        name="in_proj",
    )(h2d, g, wqkv, wrkv, wlora, mu_rkv, mu_lora, first_rkv, first_lora)


def _sb_attn_kernel(q_ref, k_ref, v_ref, km_ref, vm_ref, o_ref, qh_sc, acc_sc, rest_sc, *, tq, tk):
    i = pl.program_id(2)
    q = q_ref[0]
    low = lax.broadcasted_iota(jnp.int32, (1, LANES), 1) < HEAD_DIM
    zero = jnp.zeros_like(q)
    qh_sc[0] = jnp.where(low, q, zero)
    qh_sc[1] = jnp.where(low, zero, q)
    acc_sc[...] = jnp.zeros_like(acc_sc)
    rest_sc[...] = jnp.zeros_like(rest_sc)

    row = lax.broadcasted_iota(jnp.int32, (tk, tk), 0)
    col = lax.broadcasted_iota(jnp.int32, (tk, tk), 1)
    suffix_w = jnp.where(row > col, -1.0, 0.0).astype(BF16)
    diag_mask = col < row
    meta_mask = lax.broadcasted_iota(jnp.int32, (1, META_ROWS), 1) >= META_ROWS - N_META

    def sweep(chains, w, mask):
        if mask is not None:
            mask = jnp.concatenate([mask, mask], axis=0) if mask.shape[0] > 1 else mask
        z = [_dot_nt(jnp.concatenate([qh_sc[0, rows, :], qh_sc[1, rows, :]], axis=0), kb)
             for rows, kb, _, _ in chains]
        sp = [jnp.maximum(x, jnp.log2(1.0 + jnp.exp2(jnp.minimum(x, EXP2_CLAMP)))) for x in z]
        if mask is not None:
            sp = [jnp.where(mask, x, 0.0) for x in sp]
        sizes = [x.shape[0] for x in sp]
        cum = _dot(jnp.concatenate([x.astype(BF16) for x in sp], axis=0), w)
        cum = [cum[sum(sizes[:n]):sum(sizes[:n + 1])] for n in range(len(sp))]
        p = [jnp.exp2((a - b + c).astype(BF16)) for a, b, c in zip(z, sp, cum)]
        if mask is not None:
            p = [jnp.where(mask, x, jnp.zeros_like(x)) for x in p]
        pv = [_dot(x, c[2]) for x, c in zip(p, chains)]
        for (rows, _, _, valid), s, c, o in zip(chains, sp, cum, pv):
            n = o.shape[0] // 2
            for hh in range(2):
                half = slice(hh * n, (hh + 1) * n)
                rest = rest_sc[hh, rows, :]
                scale = jnp.exp2(rest)
                total = c[half, :1] - s[half, :1]
                if valid is not None:
                    scale = scale * valid
                    total = total * valid
                acc_sc[hh, rows, :] += scale * o[half]
                rest_sc[hh, rows, :] = rest + total

    n_blk = tq // tk

    def wave(s, mask):
        for rb0 in range(0, n_blk, SB_GROUP):
            chains = []
            for rb in range(rb0, rb0 + SB_GROUP):
                j = i * n_blk + rb - s
                valid = None if mask is not None else (j >= 0).astype(F32)
                start = pl.multiple_of(jnp.maximum(j, 0) * tk, tk)
                kb = k_ref[0, pl.ds(start, tk), :]
                vb = v_ref[0, pl.ds(start, tk), :]
                chains.append((slice(rb * tk, (rb + 1) * tk), kb, vb, valid))
            sweep(chains, suffix_w, mask)

    def stick_left(rb):
        rows = slice(rb * tk, (rb + 1) * tk)
        return jnp.maximum(jnp.max(rest_sc[0, rows, :]), jnp.max(rest_sc[1, rows, :]))

    def alive(s):
        left = [jnp.where(i * n_blk + rb - s >= 0, stick_left(rb), -jnp.inf)
                for rb in range(n_blk)]
        return functools.reduce(jnp.maximum, left) > DEAD_LOG2

    wave(0, diag_mask)
    lax.while_loop(lambda c: c[1],
                   lambda c: (wave(c[0], None), (c[0] + 1, alive(c[0] + 1)))[1],
                   (jnp.int32(1), alive(1)))

    left = [stick_left(rb) for rb in range(n_blk)]
    for rb in range(n_blk):
        @pl.when(left[rb] > DEAD_LOG2)
        def _():
            sweep([(slice(rb * tk, (rb + 1) * tk), km_ref[...], vm_ref[...], None)],
                  suffix_w[tk - META_ROWS:, tk - META_ROWS:], meta_mask)

    o_ref[0] = jnp.where(low, acc_sc[0], acc_sc[1]).astype(o_ref.dtype)


def _sb_attention(qkv, qkv_meta):
    b, l, _ = qkv.shape
    tq = min(SB_Q_TILE, l)
    tk = min(SB_K_TILE, l)
    assert l % tq == 0 and tq % (tk * SB_GROUP) == 0 and tk >= META_ROWS
    n_pairs = SB_WIDTH // LANES
    return pl.pallas_call(
        functools.partial(_sb_attn_kernel, tq=tq, tk=tk),
        grid=(b, n_pairs, l // tq),
        in_specs=[
            pl.BlockSpec((1, tq, LANES), lambda bi, hp, i: (bi, i, hp)),
            pl.BlockSpec((1, l, LANES), lambda bi, hp, i: (bi, 0, n_pairs + hp)),
            pl.BlockSpec((1, l, LANES), lambda bi, hp, i: (bi, 0, 2 * n_pairs + hp)),
            pl.BlockSpec((META_ROWS, LANES), lambda bi, hp, i: (0, n_pairs + hp)),
            pl.BlockSpec((META_ROWS, LANES), lambda bi, hp, i: (0, 2 * n_pairs + hp)),
        ],
        out_specs=pl.BlockSpec((1, tq, LANES), lambda bi, hp, i: (bi, i, hp)),
        out_shape=jax.ShapeDtypeStruct((b, l, SB_WIDTH), BF16),
        scratch_shapes=[pltpu.VMEM((2, tq, LANES), BF16), pltpu.VMEM((2, tq, LANES), F32),
                        pltpu.VMEM((2, tq, 1), F32)],
        compiler_params=pltpu.CompilerParams(
            dimension_semantics=("parallel", "parallel", "arbitrary"),
            vmem_limit_bytes=VMEM_LIMIT),
        name="sb_attention",
    )(qkv, qkv, qkv, qkv_meta, qkv_meta)


def _rwkv_kernel(rkv_ref, lora_ref, h0_ref, w0_ref, wup_ref, a0_ref, aup_ref, gup_ref, kk_ref,
                 ka_ref, rk_ref, lnw_ref, lnb_ref, y_ref, hT_ref, h_sc, r_sc, k_sc, v_sc, kk_sc,
                 a_sc, lw_sc, y_sc, *, tile):
    c_w = RWKV_WIDTH

    @pl.when(pl.program_id(1) == 0)
    def _():
        h_sc[...] = h0_ref[...]

    rr = lax.broadcasted_iota(jnp.int32, (PAIR, PAIR), 0)
    cc = lax.broadcasted_iota(jnp.int32, (PAIR, PAIR), 1)
    same_head = (rr < HEAD_DIM) == (cc < HEAD_DIM)
    pair_sum = jnp.where(same_head, 1.0, 0.0).astype(BF16)

    def head_sums(x):
        x = x.astype(BF16)
        return jnp.concatenate([_dot(x[:, PAIR * pr:PAIR * (pr + 1)], pair_sum)
                                for pr in range(N_PAIRS)], axis=1)

    def prepare(r0, n):
        rows = slice(r0, r0 + n)
        p = rkv_ref[0, rows, :]
        lora = lora_ref[0, rows, :]
        r = p[:, :c_w]
        k = p[:, c_w:2 * c_w]
        w = -jax.nn.softplus(-(w0_ref[...] + _bdot(jnp.tanh(lora), wup_ref[...]))) - 0.5
        a = jax.nn.sigmoid(a0_ref[...] + _bdot(lora, aup_ref[...]))
        kk = k * kk_ref[...]
        r_sc[rows, :] = r
        k_sc[rows, :] = k * (1.0 + (a - 1.0) * ka_ref[...])
        v_sc[rows, :] = p[:, 2 * c_w:]
        kk_sc[rows, :] = kk * lax.rsqrt(jnp.maximum(head_sums(kk * kk), 1e-24))
        a_sc[rows, :] = a
        lw_sc[rows, :] = -jnp.exp(w)
        return _bdot(jax.nn.sigmoid(lora), gup_ref[...])

    strict_lower = same_head & (cc < rr)
    lower = same_head & (cc <= rr)
    eye_mask = rr == cc
    eye = jnp.where(eye_mask, 1.0, 0.0)
    tr = lax.broadcasted_iota(jnp.int32, (CHUNK, CHUNK), 0)
    tc = lax.broadcasted_iota(jnp.int32, (CHUNK, CHUNK), 1)
    cumsum_w = jnp.where(tc <= tr, 1.0, 0.0).astype(BF16)
    zero = jnp.zeros((PAIR, PAIR), BF16)

    def chunk_operands(c):
        rows = slice(c * CHUNK, (c + 1) * CHUNK)
        lw = lw_sc[rows, :]
        lw_hi = lw.astype(BF16)
        g = _dot(cumsum_w, lw_hi) + _dot(cumsum_w, (lw - lw_hi.astype(F32)).astype(BF16))
        g_end = g[CHUNK - 1:CHUNK, :]
        decay_out = jnp.exp(-g)
        decay_end = jnp.exp(g_end - g)
        kk_c = kk_sc[rows, :]
        kka = kk_c * a_sc[rows, :]
        k_c = k_sc[rows, :]
        full = dict(
            a_bar=(-kk_c * jnp.exp(g - lw)).astype(BF16),
            r_bar=(r_sc[rows, :] * jnp.exp(g)).astype(BF16),
            b_til=(kka * decay_out).astype(BF16), k_til=(k_c * decay_out).astype(BF16),
            b_hat=(kka * decay_end).astype(BF16), k_hat=(k_c * decay_end).astype(BF16),
            v=v_sc[rows, :].astype(BF16))
        gamma_end = jnp.exp(g_end)
        masked = ("a_bar", "r_bar", "v", "b_hat", "k_hat")
        out = []
        for pr in range(N_PAIRS):
            sl = slice(PAIR * pr, PAIR * (pr + 1))
            ops = {}
            for name, x in full.items():
                x2 = jnp.concatenate([x[:, sl], x[:, sl]], axis=0)
                ops[name] = jnp.where(same_head, x2, zero) if name in masked else x2
            ops.update(rows=rows, sl=sl, pr=pr, gamma_end=gamma_end[:, sl],
                       r_pair=full["r_bar"][:, sl])
            out.append(ops)
        return out

    def state_free(chunks, out):
        chains = [ops for c in chunks for ops in chunk_operands(c)]
        s1 = [_dot_nt(jnp.concatenate([o["a_bar"], o["r_bar"]], axis=0),
                      jnp.concatenate([o["b_til"], o["k_til"]], axis=0)) for o in chains]
        n_mat = [jnp.where(strict_lower, s[:PAIR, :PAIR], 0.0) for s in s1]
        a_ak = [jnp.where(strict_lower, s[:PAIR, PAIR:], 0.0).astype(BF16) for s in s1]
        a_rb = [jnp.where(lower, s[PAIR:, :PAIR], 0.0).astype(BF16) for s in s1]
        a_rk = [jnp.where(lower, s[PAIR:, PAIR:], 0.0).astype(BF16) for s in s1]
        yield
        inv = [eye + n for n in n_mat]
        power = [n.astype(BF16) for n in n_mat]
        for _ in range(5):
            power = [_dot(p, p).astype(BF16) for p in power]
            yield
            inv = [i + _dot(i.astype(BF16), p) for i, p in zip(inv, power)]
            yield
        akv = [_dot(a, o["v"]).astype(BF16) for a, o in zip(a_ak, chains)]
        yield
        pq = [_dot(i.astype(BF16), jnp.concatenate([o["a_bar"], x], axis=1)).astype(BF16)
              for i, o, x in zip(inv, chains, akv)]
        yield
        pqv = [jnp.concatenate([x, jnp.concatenate([zero, o["v"]], axis=1)], axis=0)
               for x, o in zip(pq, chains)]
        ef = [_dot(jnp.concatenate([b, k], axis=1), x) for b, k, x in zip(a_rb, a_rk, pqv)]
        ef = [x[:CHUNK] + x[CHUNK:] for x in ef]
        yield
        mg = [_dot_tn(jnp.concatenate([o["b_hat"], o["k_hat"]], axis=0), x)
              for o, x in zip(chains, pqv)]
        out.extend(zip(chains, ef, mg))

    def advance_state(items):
        for o, x, m in items:
            e_mat = x[:, :PAIR] + o["r_pair"].astype(F32)
            m_mat = m[:, :PAIR] + jnp.where(eye_mask, o["gamma_end"], 0.0)
            eh = _bdot(jnp.concatenate([e_mat, m_mat], axis=0), h_sc[o["pr"]])
            y_sc[o["rows"], o["sl"]] = eh[:CHUNK] + x[:, PAIR:]
            h_sc[o["pr"]] = eh[CHUNK:] + m[:, PAIR:]

    def finish(r0, n, gate):
        rows = slice(r0, r0 + n)
        y = y_sc[rows, :]
        inv_n = 1.0 / HEAD_DIM
        d = y - head_sums(y) * inv_n
        yn = d * lax.rsqrt(head_sums(d * d) * inv_n + LNX_EPS)
        yn = yn * lnw_ref[...] + lnb_ref[...]
        bonus = head_sums(r_sc[rows, :] * k_sc[rows, :] * rk_ref[...]) * v_sc[rows, :]
        y_ref[0, rows, :] = ((yn + bonus) * gate).astype(y_ref.dtype)

    n_chunks = tile // CHUNK
    first = list(range((n_chunks + 1) // 2))
    second = list(range(len(first), n_chunks))
    n_first = len(first) * CHUNK
    gate_a = prepare(0, n_first)
    done_a, done_b = [], []
    stages_a = state_free(first, done_a)
    next(stages_a)
    gate_b = prepare(n_first, tile - n_first) if second else None
    for _ in stages_a:
        pass
    pending = [functools.partial(advance_state, done_a[c * N_PAIRS:(c + 1) * N_PAIRS])
               for c in range(len(first))] + [functools.partial(finish, 0, n_first, gate_a)]
    for _ in state_free(second, done_b) if second else ():
        if pending:
            pending.pop(0)()
    while pending:
        pending.pop(0)()
    for c in range(len(second)):
        advance_state(done_b[c * N_PAIRS:(c + 1) * N_PAIRS])
    if second:
        finish(n_first, tile - n_first, gate_b)
    hT_ref[0] = h_sc[...]


def _rwkv(rkv, lora, h0, params):
    b, l, _ = rkv.shape
    tile = min(RWKV_TILE, l)
    assert l % tile == 0 and tile % CHUNK == 0
    seq = lambda w: pl.BlockSpec((1, tile, w), lambda bi, t: (bi, t, 0))
    scr = lambda: pltpu.VMEM((tile, RWKV_WIDTH), F32)
    return pl.pallas_call(
        functools.partial(_rwkv_kernel, tile=tile),
        grid=(b, l // tile),
        in_specs=[seq(3 * RWKV_WIDTH), seq(LORA_PAD), _const_spec(h0.shape)]
                 + [_const_spec(p.shape) for p in params],
        out_specs=[seq(RWKV_WIDTH),
                   pl.BlockSpec((1, N_PAIRS, PAIR, PAIR), lambda bi, t: (bi, 0, 0, 0))],
        out_shape=[jax.ShapeDtypeStruct((b, l, RWKV_WIDTH), BF16),
                   jax.ShapeDtypeStruct((b, N_PAIRS, PAIR, PAIR), F32)],
        scratch_shapes=[pltpu.VMEM((N_PAIRS, PAIR, PAIR), F32),
                        scr(), scr(), scr(), scr(), scr(), scr(), scr()],
        compiler_params=pltpu.CompilerParams(dimension_semantics=("parallel", "arbitrary"),
                                             vmem_limit_bytes=VMEM_LIMIT),
        name="rwkv7",
    )(rkv, lora, h0, *params)


def _pad_rows(w, start, total):
    return jnp.pad(w, ((start, total - start - w.shape[0]), (0, 0)))


def kernel(x, meta_tokens, ffn1_norm, ffn1_w_gate, ffn1_w_up, ffn1_w_down, mix_norm, w_in, rwkv_mu, rwkv_w0, rwkv_w_up, rwkv_a0, rwkv_a_up, rwkv_g_up, rwkv_k_k, rwkv_k_a, rwkv_r_k, rwkv_lnx_w, rwkv_lnx_b, w_out, ffn2_norm, ffn2_w_gate, ffn2_w_up, ffn2_w_down, final_norm):
    b, l, _ = x.shape
    depth = ffn1_norm.shape[0]
    assert depth == 1, "stacked layers would need the meta rows carried through the mixers"
    row = lambda p: p.reshape(1, -1).astype(F32)
    n_rkv = 3 * RWKV_WIDTH
    n_lora = W_LORA + A_LORA + G_LORA

    h = x.reshape(b * l, D_MODEL)
    h_meta = _pad_rows(meta_tokens.astype(F32), META_ROWS - N_META, META_ROWS)
    for d in range(depth):
        w_qkv = w_in[d][:, :3 * SB_WIDTH]
        w_qkv = w_qkv.at[:, :SB_WIDTH].multiply(HEAD_DIM ** -0.5 * LOG2_E).astype(BF16)
        w_rkv = w_in[d][:, 3 * SB_WIDTH:3 * SB_WIDTH + n_rkv].astype(BF16)
        w_lora = jnp.pad(w_in[d][:, 3 * SB_WIDTH + n_rkv:],
                         ((0, 0), (0, LORA_PAD - n_lora))).astype(BF16)
        mu_rkv = row(rwkv_mu[d][:n_rkv])
        mu_lora = jnp.pad(row(rwkv_mu[d][n_rkv:]), ((0, 0), (0, LORA_PAD - n_lora)))
        rwkv_params = (
            row(rwkv_w0[d]), _pad_rows(rwkv_w_up[d], 0, LORA_PAD),
            row(rwkv_a0[d]), _pad_rows(rwkv_a_up[d], W_LORA, LORA_PAD),
            _pad_rows(rwkv_g_up[d], W_LORA + A_LORA, LORA_PAD), row(rwkv_k_k[d]),
            row(rwkv_k_a[d]), row(rwkv_r_k[d]), row(rwkv_lnx_w[d]), row(rwkv_lnx_b[d]))
        ffn1_w = (row(ffn1_norm[d]), ffn1_w_gate[d].astype(BF16), ffn1_w_up[d].astype(BF16),
                  ffn1_w_down[d].astype(BF16))
        ffn2_w = (row(ffn2_norm[d]), ffn2_w_gate[d].astype(BF16), ffn2_w_up[d].astype(BF16),
                  ffn2_w_down[d].astype(BF16))
        wo_sb = w_out[d][:SB_WIDTH].astype(BF16)
        wo_rw = w_out[d][SB_WIDTH:].astype(BF16)
        gf = row(final_norm)

        h = _ffn1(h, *ffn1_w)
        h_meta = _ffn1(h_meta, *ffn1_w)
        in_w = (row(mix_norm[d]), w_qkv, w_rkv, w_lora, mu_rkv, mu_lora)
        qkv_m, rkv_m, lora_m, last_rkv_m, last_lora_m = _in_proj(
            h_meta, META_ROWS, *in_w, jnp.zeros((1, n_rkv), F32), jnp.zeros((1, LORA_PAD), F32))
        qkv, rkv, lora, _, _ = _in_proj(h, l, *in_w, last_rkv_m, last_lora_m)

        sb = _sb_attention(qkv.reshape(b, l, -1), qkv_m)

        zero_state = jnp.zeros((N_PAIRS, PAIR, PAIR), F32)
        _, h_meta_state = _rwkv(rkv_m[None], lora_m[None], zero_state, rwkv_params)
        rw, _ = _rwkv(rkv.reshape(b, l, -1), lora.reshape(b, l, -1), h_meta_state[0],
                      rwkv_params)

        h = _ffn2(h, sb.reshape(b * l, -1), rw.reshape(b * l, -1), wo_sb, wo_rw, *ffn2_w, gf)
    return h.reshape(b, l, D_MODEL)
```

```python
import functools

import jax
import jax.numpy as jnp
from jax import lax
from jax.experimental import pallas as pl
from jax.experimental.pallas import tpu as pltpu

D_MODEL = 1024
N_META = 16
HEAD_DIM = 64
SB_WIDTH = 512
RWKV_WIDTH = 512
D_FF = 2816
W_LORA = 32
A_LORA = 32
G_LORA = 96
RMS_EPS = 1e-6
LNX_EPS = 64e-5

LANES = 128
META_ROWS = 128
LORA_PAD = 256
CHUNK = 64
PAIR = 2 * HEAD_DIM
N_PAIRS = RWKV_WIDTH // PAIR
VMEM_LIMIT = 56 * 1024 * 1024
TOKEN_TILE = 512
RWKV_TILE = 512
SB_Q_TILE = 4096
SB_K_TILE = 256
SB_GROUP = 2

F32 = jnp.float32
BF16 = jnp.bfloat16
LOG2_E = 1.4426950408889634
EXP2_CLAMP = 126.0
DEAD_LOG2 = -150.0


def _const_spec(shape):
    zeros = (0,) * len(shape)
    return pl.BlockSpec(shape, lambda *_: zeros, pipeline_mode=pl.Buffered(1))


def _rms(x, g):
    ms = jnp.mean(x * x, axis=-1, keepdims=True)
    return x * lax.rsqrt(ms + RMS_EPS) * g


def _dot(a, b):
    return jnp.dot(a, b, preferred_element_type=F32)


def _bdot(a, b):
    return _dot(a.astype(BF16), b.astype(BF16))


def _dot_nt(a, b):
    return lax.dot_general(a, b, (((1,), (1,)), ((), ())), preferred_element_type=F32)


def _dot_tn(a, b):
    return lax.dot_general(a, b, (((0,), (0,)), ((), ())), preferred_element_type=F32)


def _swiglu_half_step(h, g, wg_ref, wu_ref, wd_ref):
    n = _rms(h, g).astype(BF16)
    gate = _dot(n, wg_ref[...])
    up = _dot(n, wu_ref[...])
    act = (gate * jax.nn.sigmoid(gate) * up).astype(BF16)
    return h + 0.5 * _dot(act, wd_ref[...])


def _ffn1_kernel(x_ref, g_ref, wg_ref, wu_ref, wd_ref, o_ref):
    o_ref[...] = _swiglu_half_step(x_ref[...], g_ref[...], wg_ref, wu_ref, wd_ref)


def _ffn2_kernel(h_ref, sb_ref, rw_ref, wo_sb_ref, wo_rw_ref, g_ref, wg_ref, wu_ref, wd_ref,
                 gf_ref, o_ref):
    h = h_ref[...] + _dot(sb_ref[...], wo_sb_ref[...]) + _dot(rw_ref[...], wo_rw_ref[...])
    h = _swiglu_half_step(h, g_ref[...], wg_ref, wu_ref, wd_ref)
    o_ref[...] = _rms(h, gf_ref[...])


def _in_proj_kernel(h_ref, g_ref, wqkv_ref, wrkv_ref, wlora_ref, mu_rkv_ref, mu_lora_ref,
                    first_rkv_ref, first_lora_ref, qkv_ref, rkv_ref, lora_ref, last_rkv_ref,
                    last_lora_ref, prev_rkv_sc, prev_lora_sc, *, tiles_per_seq):
    tile = h_ref.shape[0]

    @pl.when(pl.program_id(0) % tiles_per_seq == 0)
    def _():
        prev_rkv_sc[...] = first_rkv_ref[...]
        prev_lora_sc[...] = first_lora_ref[...]

    n = _rms(h_ref[...], g_ref[...]).astype(BF16)
    qkv_ref[...] = _dot(n, wqkv_ref[...]).astype(BF16)
    first_row = lax.broadcasted_iota(jnp.int32, (tile, 1), 0) == 0

    def shifted(w_ref, mu_ref, prev_sc, out_ref, last_ref):
        x = _dot(n, w_ref[...])
        x_prev = jnp.where(first_row, prev_sc[...], pltpu.roll(x, 1, axis=0))
        out_ref[...] = x + (x_prev - x) * mu_ref[...]
        prev_sc[...] = x[tile - 1:tile, :]
        last_ref[...] = x[tile - 1:tile, :]

    shifted(wrkv_ref, mu_rkv_ref, prev_rkv_sc, rkv_ref, last_rkv_ref)
    shifted(wlora_ref, mu_lora_ref, prev_lora_sc, lora_ref, last_lora_ref)


def _token_tile(n_tokens):
    t = min(TOKEN_TILE, n_tokens)
    assert n_tokens % t == 0 and t % 8 == 0, (n_tokens, t)
    return t


def _row_spec(t, width):
    return pl.BlockSpec((t, width), lambda i: (i, 0))


def _token_params():
    return pltpu.CompilerParams(dimension_semantics=("parallel",), vmem_limit_bytes=VMEM_LIMIT)


def _ffn1(x2d, g, wg, wu, wd):
    n = x2d.shape[0]
    t = _token_tile(n)
    return pl.pallas_call(
        _ffn1_kernel,
        grid=(n // t,),
        in_specs=[_row_spec(t, D_MODEL), _const_spec((1, D_MODEL)), _const_spec(wg.shape),
                  _const_spec(wu.shape), _const_spec(wd.shape)],
        out_specs=_row_spec(t, D_MODEL),
        out_shape=jax.ShapeDtypeStruct((n, D_MODEL), F32),
        compiler_params=_token_params(),
        name="ffn1",
    )(x2d, g, wg, wu, wd)


def _ffn2(h2d, sb2d, rw2d, wo_sb, wo_rw, g, wg, wu, wd, gf):
    n = h2d.shape[0]
    t = _token_tile(n)
    return pl.pallas_call(
        _ffn2_kernel,
        grid=(n // t,),
        in_specs=[_row_spec(t, D_MODEL), _row_spec(t, SB_WIDTH), _row_spec(t, RWKV_WIDTH),
                  _const_spec(wo_sb.shape), _const_spec(wo_rw.shape), _const_spec((1, D_MODEL)),
                  _const_spec(wg.shape), _const_spec(wu.shape), _const_spec(wd.shape),
                  _const_spec((1, D_MODEL))],
        out_specs=_row_spec(t, D_MODEL),
        out_shape=jax.ShapeDtypeStruct((n, D_MODEL), F32),
        compiler_params=_token_params(),
        name="ffn2",
    )(h2d, sb2d, rw2d, wo_sb, wo_rw, g, wg, wu, wd, gf)


def _in_proj(h2d, seq_len, g, wqkv, wrkv, wlora, mu_rkv, mu_lora, first_rkv, first_lora):
    n = h2d.shape[0]
    t = _token_tile(seq_len)
    n_rkv = 3 * RWKV_WIDTH
    last = lambda w: pl.BlockSpec((1, w), lambda i: (0, 0))
    return pl.pallas_call(
        functools.partial(_in_proj_kernel, tiles_per_seq=seq_len // t),
        grid=(n // t,),
        in_specs=[_row_spec(t, D_MODEL), _const_spec((1, D_MODEL)), _const_spec(wqkv.shape),
                  _const_spec(wrkv.shape), _const_spec(wlora.shape), _const_spec(mu_rkv.shape),
                  _const_spec(mu_lora.shape), _const_spec(first_rkv.shape),
                  _const_spec(first_lora.shape)],
        out_specs=[_row_spec(t, 3 * SB_WIDTH), _row_spec(t, n_rkv), _row_spec(t, LORA_PAD),
                   last(n_rkv), last(LORA_PAD)],
        out_shape=[jax.ShapeDtypeStruct((n, 3 * SB_WIDTH), BF16),
                   jax.ShapeDtypeStruct((n, n_rkv), F32),
                   jax.ShapeDtypeStruct((n, LORA_PAD), F32),
                   jax.ShapeDtypeStruct((1, n_rkv), F32),
                   jax.ShapeDtypeStruct((1, LORA_PAD), F32)],
        scratch_shapes=[pltpu.VMEM((1, n_rkv), F32), pltpu.VMEM((1, LORA_PAD), F32)],
        compiler_params=pltpu.CompilerParams(dimension_semantics=("arbitrary",),
                                             vmem_limit_bytes=VMEM_LIMIT),
        name="in_proj",
    )(h2d, g, wqkv, wrkv, wlora, mu_rkv, mu_lora, first_rkv, first_lora)


def _sb_attn_kernel(q_ref, k_ref, v_ref, km_ref, vm_ref, o_ref, qh_sc, acc_sc, rest_sc, *, tq, tk):
    i = pl.program_id(2)
    q = q_ref[0]
    low = lax.broadcasted_iota(jnp.int32, (1, LANES), 1) < HEAD_DIM
    zero = jnp.zeros_like(q)
    qh_sc[0] = jnp.where(low, q, zero)
    qh_sc[1] = jnp.where(low, zero, q)
    acc_sc[...] = jnp.zeros_like(acc_sc)
    rest_sc[...] = jnp.zeros_like(rest_sc)

    row = lax.broadcasted_iota(jnp.int32, (tk, tk), 0)
    col = lax.broadcasted_iota(jnp.int32, (tk, tk), 1)
    suffix_w = jnp.where(row > col, -1.0, 0.0).astype(BF16)
    diag_mask = col < row
    meta_mask = lax.broadcasted_iota(jnp.int32, (1, META_ROWS), 1) >= META_ROWS - N_META

    def sweep(chains, w, mask):
        if mask is not None:
            mask = jnp.concatenate([mask, mask], axis=0) if mask.shape[0] > 1 else mask
        z = [_dot_nt(jnp.concatenate([qh_sc[0, rows, :], qh_sc[1, rows, :]], axis=0), kb)
             for rows, kb, _, _ in chains]
        sp = [jnp.maximum(x, jnp.log2(1.0 + jnp.exp2(jnp.minimum(x, EXP2_CLAMP)))) for x in z]
        if mask is not None:
            sp = [jnp.where(mask, x, 0.0) for x in sp]
        sizes = [x.shape[0] for x in sp]
        cum = _dot(jnp.concatenate([x.astype(BF16) for x in sp], axis=0), w)
        cum = [cum[sum(sizes[:n]):sum(sizes[:n + 1])] for n in range(len(sp))]
        p = [jnp.exp2((a - b + c).astype(BF16)) for a, b, c in zip(z, sp, cum)]
        if mask is not None:
            p = [jnp.where(mask, x, jnp.zeros_like(x)) for x in p]
        pv = [_dot(x, c[2]) for x, c in zip(p, chains)]
        for (rows, _, _, valid), s, c, o in zip(chains, sp, cum, pv):
            n = o.shape[0] // 2
            for hh in range(2):
                half = slice(hh * n, (hh + 1) * n)
                rest = rest_sc[hh, rows, :]
                scale = jnp.exp2(rest)
                total = c[half, :1] - s[half, :1]
                if valid is not None:
                    scale = scale * valid
                    total = total * valid
                acc_sc[hh, rows, :] += scale * o[half]
                rest_sc[hh, rows, :] = rest + total

    n_blk = tq // tk

    def wave(s, mask):
        for rb0 in range(0, n_blk, SB_GROUP):
            chains = []
            for rb in range(rb0, rb0 + SB_GROUP):
                j = i * n_blk + rb - s
                valid = None if mask is not None else (j >= 0).astype(F32)
                start = pl.multiple_of(jnp.maximum(j, 0) * tk, tk)
                kb = k_ref[0, pl.ds(start, tk), :]
                vb = v_ref[0, pl.ds(start, tk), :]
                chains.append((slice(rb * tk, (rb + 1) * tk), kb, vb, valid))
            sweep(chains, suffix_w, mask)

    def stick_left(rb):
        rows = slice(rb * tk, (rb + 1) * tk)
        return jnp.maximum(jnp.max(rest_sc[0, rows, :]), jnp.max(rest_sc[1, rows, :]))

    def alive(s):
        left = [jnp.where(i * n_blk + rb - s >= 0, stick_left(rb), -jnp.inf)
                for rb in range(n_blk)]
        return functools.reduce(jnp.maximum, left) > DEAD_LOG2

    wave(0, diag_mask)
    lax.while_loop(lambda c: c[1],
                   lambda c: (wave(c[0], None), (c[0] + 1, alive(c[0] + 1)))[1],
                   (jnp.int32(1), alive(1)))

    left = [stick_left(rb) for rb in range(n_blk)]
    for rb in range(n_blk):
        @pl.when(left[rb] > DEAD_LOG2)
        def _():
            sweep([(slice(rb * tk, (rb + 1) * tk), km_ref[...], vm_ref[...], None)],
                  suffix_w[tk - META_ROWS:, tk - META_ROWS:], meta_mask)

    o_ref[0] = jnp.where(low, acc_sc[0], acc_sc[1]).astype(o_ref.dtype)


def _sb_attention(qkv, qkv_meta):
    b, l, _ = qkv.shape
    tq = min(SB_Q_TILE, l)
    tk = min(SB_K_TILE, l)
    assert l % tq == 0 and tq % (tk * SB_GROUP) == 0 and tk >= META_ROWS
    n_pairs = SB_WIDTH // LANES
    return pl.pallas_call(
        functools.partial(_sb_attn_kernel, tq=tq, tk=tk),
        grid=(b, n_pairs, l // tq),
        in_specs=[
            pl.BlockSpec((1, tq, LANES), lambda bi, hp, i: (bi, i, hp)),
            pl.BlockSpec((1, l, LANES), lambda bi, hp, i: (bi, 0, n_pairs + hp)),
            pl.BlockSpec((1, l, LANES), lambda bi, hp, i: (bi, 0, 2 * n_pairs + hp)),
            pl.BlockSpec((META_ROWS, LANES), lambda bi, hp, i: (0, n_pairs + hp)),
            pl.BlockSpec((META_ROWS, LANES), lambda bi, hp, i: (0, 2 * n_pairs + hp)),
        ],
        out_specs=pl.BlockSpec((1, tq, LANES), lambda bi, hp, i: (bi, i, hp)),
        out_shape=jax.ShapeDtypeStruct((b, l, SB_WIDTH), BF16),
        scratch_shapes=[pltpu.VMEM((2, tq, LANES), BF16), pltpu.VMEM((2, tq, LANES), F32),
                        pltpu.VMEM((2, tq, 1), F32)],
        compiler_params=pltpu.CompilerParams(
            dimension_semantics=("parallel", "parallel", "arbitrary"),
            vmem_limit_bytes=VMEM_LIMIT),
        name="sb_attention",
    )(qkv, qkv, qkv, qkv_meta, qkv_meta)


def _rwkv_kernel(rkv_ref, lora_ref, h0_ref, w0_ref, wup_ref, a0_ref, aup_ref, gup_ref, kk_ref,
                 ka_ref, rk_ref, lnw_ref, lnb_ref, y_ref, hT_ref, h_sc, r_sc, k_sc, v_sc, kk_sc,
                 a_sc, lw_sc, y_sc, *, tile):
    c_w = RWKV_WIDTH

    @pl.when(pl.program_id(1) == 0)
    def _():
        h_sc[...] = h0_ref[...]

    rr = lax.broadcasted_iota(jnp.int32, (PAIR, PAIR), 0)
    cc = lax.broadcasted_iota(jnp.int32, (PAIR, PAIR), 1)
    same_head = (rr < HEAD_DIM) == (cc < HEAD_DIM)
    pair_sum = jnp.where(same_head, 1.0, 0.0).astype(BF16)

    def head_sums(x):
        x = x.astype(BF16)
        return jnp.concatenate([_dot(x[:, PAIR * pr:PAIR * (pr + 1)], pair_sum)
                                for pr in range(N_PAIRS)], axis=1)

    def prepare(r0, n):
        rows = slice(r0, r0 + n)
        p = rkv_ref[0, rows, :]
        lora = lora_ref[0, rows, :]
        r = p[:, :c_w]
        k = p[:, c_w:2 * c_w]
        w = -jax.nn.softplus(-(w0_ref[...] + _bdot(jnp.tanh(lora), wup_ref[...]))) - 0.5
        a = jax.nn.sigmoid(a0_ref[...] + _bdot(lora, aup_ref[...]))
        kk = k * kk_ref[...]
        r_sc[rows, :] = r
        k_sc[rows, :] = k * (1.0 + (a - 1.0) * ka_ref[...])
        v_sc[rows, :] = p[:, 2 * c_w:]
        kk_sc[rows, :] = kk * lax.rsqrt(jnp.maximum(head_sums(kk * kk), 1e-24))
        a_sc[rows, :] = a
        lw_sc[rows, :] = -jnp.exp(w)
        return _bdot(jax.nn.sigmoid(lora), gup_ref[...])

    strict_lower = same_head & (cc < rr)
    lower = same_head & (cc <= rr)
    eye_mask = rr == cc
    eye = jnp.where(eye_mask, 1.0, 0.0)
    tr = lax.broadcasted_iota(jnp.int32, (CHUNK, CHUNK), 0)
    tc = lax.broadcasted_iota(jnp.int32, (CHUNK, CHUNK), 1)
    cumsum_w = jnp.where(tc <= tr, 1.0, 0.0).astype(BF16)
    zero = jnp.zeros((PAIR, PAIR), BF16)

    def chunk_operands(c):
        rows = slice(c * CHUNK, (c + 1) * CHUNK)
        lw = lw_sc[rows, :]
        lw_hi = lw.astype(BF16)
        g = _dot(cumsum_w, lw_hi) + _dot(cumsum_w, (lw - lw_hi.astype(F32)).astype(BF16))
        g_end = g[CHUNK - 1:CHUNK, :]
        decay_out = jnp.exp(-g)
        decay_end = jnp.exp(g_end - g)
        kk_c = kk_sc[rows, :]
        kka = kk_c * a_sc[rows, :]
        k_c = k_sc[rows, :]
        full = dict(
            a_bar=(-kk_c * jnp.exp(g - lw)).astype(BF16),
            r_bar=(r_sc[rows, :] * jnp.exp(g)).astype(BF16),
            b_til=(kka * decay_out).astype(BF16), k_til=(k_c * decay_out).astype(BF16),
            b_hat=(kka * decay_end).astype(BF16), k_hat=(k_c * decay_end).astype(BF16),
            v=v_sc[rows, :].astype(BF16))
        gamma_end = jnp.exp(g_end)
        masked = ("a_bar", "r_bar", "v", "b_hat", "k_hat")
        out = []
        for pr in range(N_PAIRS):
            sl = slice(PAIR * pr, PAIR * (pr + 1))
            ops = {}
            for name, x in full.items():
                x2 = jnp.concatenate([x[:, sl], x[:, sl]], axis=0)
                ops[name] = jnp.where(same_head, x2, zero) if name in masked else x2
            ops.update(rows=rows, sl=sl, pr=pr, gamma_end=gamma_end[:, sl],
                       r_pair=full["r_bar"][:, sl])
            out.append(ops)
        return out

    def state_free(chunks, out):
        chains = [ops for c in chunks for ops in chunk_operands(c)]
        s1 = [_dot_nt(jnp.concatenate([o["a_bar"], o["r_bar"]], axis=0),
                      jnp.concatenate([o["b_til"], o["k_til"]], axis=0)) for o in chains]
        n_mat = [jnp.where(strict_lower, s[:PAIR, :PAIR], 0.0) for s in s1]
        a_ak = [jnp.where(strict_lower, s[:PAIR, PAIR:], 0.0).astype(BF16) for s in s1]
        a_rb = [jnp.where(lower, s[PAIR:, :PAIR], 0.0).astype(BF16) for s in s1]
        a_rk = [jnp.where(lower, s[PAIR:, PAIR:], 0.0).astype(BF16) for s in s1]
        yield
        inv = [eye + n for n in n_mat]
        power = [n.astype(BF16) for n in n_mat]
        for _ in range(5):
            power = [_dot(p, p).astype(BF16) for p in power]
            yield
            inv = [i + _dot(i.astype(BF16), p) for i, p in zip(inv, power)]
            yield
        akv = [_dot(a, o["v"]).astype(BF16) for a, o in zip(a_ak, chains)]
        yield
        pq = [_dot(i.astype(BF16), jnp.concatenate([o["a_bar"], x], axis=1)).astype(BF16)
              for i, o, x in zip(inv, chains, akv)]
        yield
        pqv = [jnp.concatenate([x, jnp.concatenate([zero, o["v"]], axis=1)], axis=0)
               for x, o in zip(pq, chains)]
        ef = [_dot(jnp.concatenate([b, k], axis=1), x) for b, k, x in zip(a_rb, a_rk, pqv)]
        ef = [x[:CHUNK] + x[CHUNK:] for x in ef]
        yield
        mg = [_dot_tn(jnp.concatenate([o["b_hat"], o["k_hat"]], axis=0), x)
              for o, x in zip(chains, pqv)]
        out.extend(zip(chains, ef, mg))

    def advance_state(items):
        for o, x, m in items:
            e_mat = x[:, :PAIR] + o["r_pair"].astype(F32)
            m_mat = m[:, :PAIR] + jnp.where(eye_mask, o["gamma_end"], 0.0)
            eh = _bdot(jnp.concatenate([e_mat, m_mat], axis=0), h_sc[o["pr"]])
            y_sc[o["rows"], o["sl"]] = eh[:CHUNK] + x[:, PAIR:]
            h_sc[o["pr"]] = eh[CHUNK:] + m[:, PAIR:]

    def finish(r0, n, gate):
        rows = slice(r0, r0 + n)
        y = y_sc[rows, :]
        inv_n = 1.0 / HEAD_DIM
        d = y - head_sums(y) * inv_n
        yn = d * lax.rsqrt(head_sums(d * d) * inv_n + LNX_EPS)
        yn = yn * lnw_ref[...] + lnb_ref[...]
        bonus = head_sums(r_sc[rows, :] * k_sc[rows, :] * rk_ref[...]) * v_sc[rows, :]
        y_ref[0, rows, :] = ((yn + bonus) * gate).astype(y_ref.dtype)

    n_chunks = tile // CHUNK
    first = list(range((n_chunks + 1) // 2))
    second = list(range(len(first), n_chunks))
    n_first = len(first) * CHUNK
    gate_a = prepare(0, n_first)
    done_a, done_b = [], []
    stages_a = state_free(first, done_a)
    next(stages_a)
    gate_b = prepare(n_first, tile - n_first) if second else None
    for _ in stages_a:
        pass
    pending = [functools.partial(advance_state, done_a[c * N_PAIRS:(c + 1) * N_PAIRS])
               for c in range(len(first))] + [functools.partial(finish, 0, n_first, gate_a)]
    for _ in state_free(second, done_b) if second else ():
        if pending:
            pending.pop(0)()
    while pending:
        pending.pop(0)()
    for c in range(len(second)):
        advance_state(done_b[c * N_PAIRS:(c + 1) * N_PAIRS])
    if second:
        finish(n_first, tile - n_first, gate_b)
    hT_ref[0] = h_sc[...]


def _rwkv(rkv, lora, h0, params):
    b, l, _ = rkv.shape
    tile = min(RWKV_TILE, l)
    assert l % tile == 0 and tile % CHUNK == 0
    seq = lambda w: pl.BlockSpec((1, tile, w), lambda bi, t: (bi, t, 0))
    scr = lambda: pltpu.VMEM((tile, RWKV_WIDTH), F32)
    return pl.pallas_call(
        functools.partial(_rwkv_kernel, tile=tile),
        grid=(b, l // tile),
        in_specs=[seq(3 * RWKV_WIDTH), seq(LORA_PAD), _const_spec(h0.shape)]
                 + [_const_spec(p.shape) for p in params],
        out_specs=[seq(RWKV_WIDTH),
                   pl.BlockSpec((1, N_PAIRS, PAIR, PAIR), lambda bi, t: (bi, 0, 0, 0))],
        out_shape=[jax.ShapeDtypeStruct((b, l, RWKV_WIDTH), BF16),
                   jax.ShapeDtypeStruct((b, N_PAIRS, PAIR, PAIR), F32)],
        scratch_shapes=[pltpu.VMEM((N_PAIRS, PAIR, PAIR), F32),
                        scr(), scr(), scr(), scr(), scr(), scr(), scr()],
        compiler_params=pltpu.CompilerParams(dimension_semantics=("parallel", "arbitrary"),
                                             vmem_limit_bytes=VMEM_LIMIT),
        name="rwkv7",
    )(rkv, lora, h0, *params)


def _pad_rows(w, start, total):
    return jnp.pad(w, ((start, total - start - w.shape[0]), (0, 0)))


def kernel(x, meta_tokens, ffn1_norm, ffn1_w_gate, ffn1_w_up, ffn1_w_down, mix_norm, w_in, rwkv_mu, rwkv_w0, rwkv_w_up, rwkv_a0, rwkv_a_up, rwkv_g_up, rwkv_k_k, rwkv_k_a, rwkv_r_k, rwkv_lnx_w, rwkv_lnx_b, w_out, ffn2_norm, ffn2_w_gate, ffn2_w_up, ffn2_w_down, final_norm):
    b, l, _ = x.shape
    depth = ffn1_norm.shape[0]
    assert depth == 1, "stacked layers would need the meta rows carried through the mixers"
    row = lambda p: p.reshape(1, -1).astype(F32)
    n_rkv = 3 * RWKV_WIDTH
    n_lora = W_LORA + A_LORA + G_LORA

    h = x.reshape(b * l, D_MODEL)
    h_meta = _pad_rows(meta_tokens.astype(F32), META_ROWS - N_META, META_ROWS)
    for d in range(depth):
        w_qkv = w_in[d][:, :3 * SB_WIDTH]
        w_qkv = w_qkv.at[:, :SB_WIDTH].multiply(HEAD_DIM ** -0.5 * LOG2_E).astype(BF16)
        w_rkv = w_in[d][:, 3 * SB_WIDTH:3 * SB_WIDTH + n_rkv].astype(BF16)
        w_lora = jnp.pad(w_in[d][:, 3 * SB_WIDTH + n_rkv:],
                         ((0, 0), (0, LORA_PAD - n_lora))).astype(BF16)
        mu_rkv = row(rwkv_mu[d][:n_rkv])
        mu_lora = jnp.pad(row(rwkv_mu[d][n_rkv:]), ((0, 0), (0, LORA_PAD - n_lora)))
        rwkv_params = (
            row(rwkv_w0[d]), _pad_rows(rwkv_w_up[d], 0, LORA_PAD),
            row(rwkv_a0[d]), _pad_rows(rwkv_a_up[d], W_LORA, LORA_PAD),
            _pad_rows(rwkv_g_up[d], W_LORA + A_LORA, LORA_PAD), row(rwkv_k_k[d]),
            row(rwkv_k_a[d]), row(rwkv_r_k[d]), row(rwkv_lnx_w[d]), row(rwkv_lnx_b[d]))
        ffn1_w = (row(ffn1_norm[d]), ffn1_w_gate[d].astype(BF16), ffn1_w_up[d].astype(BF16),
                  ffn1_w_down[d].astype(BF16))
        ffn2_w = (row(ffn2_norm[d]), ffn2_w_gate[d].astype(BF16), ffn2_w_up[d].astype(BF16),
                  ffn2_w_down[d].astype(BF16))
        wo_sb = w_out[d][:SB_WIDTH].astype(BF16)
        wo_rw = w_out[d][SB_WIDTH:].astype(BF16)
        gf = row(final_norm)

        h = _ffn1(h, *ffn1_w)
        h_meta = _ffn1(h_meta, *ffn1_w)
        in_w = (row(mix_norm[d]), w_qkv, w_rkv, w_lora, mu_rkv, mu_lora)
        qkv_m, rkv_m, lora_m, last_rkv_m, last_lora_m = _in_proj(
            h_meta, META_ROWS, *in_w, jnp.zeros((1, n_rkv), F32), jnp.zeros((1, LORA_PAD), F32))
        qkv, rkv, lora, _, _ = _in_proj(h, l, *in_w, last_rkv_m, last_lora_m)

        sb = _sb_attention(qkv.reshape(b, l, -1), qkv_m)

        zero_state = jnp.zeros((N_PAIRS, PAIR, PAIR), F32)
        _, h_meta_state = _rwkv(rkv_m[None], lora_m[None], zero_state, rwkv_params)
        rw, _ = _rwkv(rkv.reshape(b, l, -1), lora.reshape(b, l, -1), h_meta_state[0],
                      rwkv_params)

        h = _ffn2(h, sb.reshape(b * l, -1), rw.reshape(b * l, -1), wo_sb, wo_rw, *ffn2_w, gf)
    return h.reshape(b, l, D_MODEL)
```

```python
import functools

import jax
import jax.numpy as jnp
from jax import lax
from jax.experimental import pallas as pl
from jax.experimental.pallas import tpu as pltpu

D_MODEL = 1024
N_META = 16
HEAD_DIM = 64
SB_WIDTH = 512
RWKV_WIDTH = 512
D_FF = 2816
W_LORA = 32
A_LORA = 32
G_LORA = 96
RMS_EPS = 1e-6
LNX_EPS = 64e-5

LANES = 128
META_ROWS = 128
LORA_PAD = 256
CHUNK = 64
PAIR = 2 * HEAD_DIM
N_PAIRS = RWKV_WIDTH // PAIR
VMEM_LIMIT = 56 * 1024 * 1024
TOKEN_TILE = 512
RWKV_TILE = 512
SB_Q_TILE = 4096
SB_K_TILE = 256
SB_GROUP = 2

F32 = jnp.float32
BF16 = jnp.bfloat16
LOG2_E = 1.4426950408889634
EXP2_CLAMP = 126.0
DEAD_LOG2 = -150.0


def _const_spec(shape):
    zeros = (0,) * len(shape)
    return pl.BlockSpec(shape, lambda *_: zeros, pipeline_mode=pl.Buffered(1))


def _rms(x, g):
    ms = jnp.mean(x * x, axis=-1, keepdims=True)
    return x * lax.rsqrt(ms + RMS_EPS) * g


def _dot(a, b):
    return jnp.dot(a, b, preferred_element_type=F32)


def _bdot(a, b):
    return _dot(a.astype(BF16), b.astype(BF16))


def _dot_nt(a, b):
    return lax.dot_general(a, b, (((1,), (1,)), ((), ())), preferred_element_type=F32)


def _dot_tn(a, b):
    return lax.dot_general(a, b, (((0,), (0,)), ((), ())), preferred_element_type=F32)


def _swiglu_half_step(h, g, wg_ref, wu_ref, wd_ref):
    n = _rms(h, g).astype(BF16)
    gate = _dot(n, wg_ref[...])
    up = _dot(n, wu_ref[...])
    act = (gate * jax.nn.sigmoid(gate) * up).astype(BF16)
    return h + 0.5 * _dot(act, wd_ref[...])


def _ffn1_kernel(x_ref, g_ref, wg_ref, wu_ref, wd_ref, o_ref):
    o_ref[...] = _swiglu_half_step(x_ref[...], g_ref[...], wg_ref, wu_ref, wd_ref)


def _ffn2_kernel(h_ref, sb_ref, rw_ref, wo_sb_ref, wo_rw_ref, g_ref, wg_ref, wu_ref, wd_ref,
                 gf_ref, o_ref):
    h = h_ref[...] + _dot(sb_ref[...], wo_sb_ref[...]) + _dot(rw_ref[...], wo_rw_ref[...])
    h = _swiglu_half_step(h, g_ref[...], wg_ref, wu_ref, wd_ref)
    o_ref[...] = _rms(h, gf_ref[...])


def _in_proj_kernel(h_ref, g_ref, wqkv_ref, wrkv_ref, wlora_ref, mu_rkv_ref, mu_lora_ref,
                    first_rkv_ref, first_lora_ref, qkv_ref, rkv_ref, lora_ref, last_rkv_ref,
                    last_lora_ref, prev_rkv_sc, prev_lora_sc, *, tiles_per_seq):
    tile = h_ref.shape[0]

    @pl.when(pl.program_id(0) % tiles_per_seq == 0)
    def _():
        prev_rkv_sc[...] = first_rkv_ref[...]
        prev_lora_sc[...] = first_lora_ref[...]

    n = _rms(h_ref[...], g_ref[...]).astype(BF16)
    qkv_ref[...] = _dot(n, wqkv_ref[...]).astype(BF16)
    first_row = lax.broadcasted_iota(jnp.int32, (tile, 1), 0) == 0

    def shifted(w_ref, mu_ref, prev_sc, out_ref, last_ref):
        x = _dot(n, w_ref[...])
        x_prev = jnp.where(first_row, prev_sc[...], pltpu.roll(x, 1, axis=0))
        out_ref[...] = x + (x_prev - x) * mu_ref[...]
        prev_sc[...] = x[tile - 1:tile, :]
        last_ref[...] = x[tile - 1:tile, :]

    shifted(wrkv_ref, mu_rkv_ref, prev_rkv_sc, rkv_ref, last_rkv_ref)
    shifted(wlora_ref, mu_lora_ref, prev_lora_sc, lora_ref, last_lora_ref)


def _token_tile(n_tokens):
    t = min(TOKEN_TILE, n_tokens)
    assert n_tokens % t == 0 and t % 8 == 0, (n_tokens, t)
    return t


def _row_spec(t, width):
    return pl.BlockSpec((t, width), lambda i: (i, 0))


def _token_params():
    return pltpu.CompilerParams(dimension_semantics=("parallel",), vmem_limit_bytes=VMEM_LIMIT)


def _ffn1(x2d, g, wg, wu, wd):
    n = x2d.shape[0]
    t = _token_tile(n)
    return pl.pallas_call(
        _ffn1_kernel,
        grid=(n // t,),
        in_specs=[_row_spec(t, D_MODEL), _const_spec((1, D_MODEL)), _const_spec(wg.shape),
                  _const_spec(wu.shape), _const_spec(wd.shape)],
        out_specs=_row_spec(t, D_MODEL),
        out_shape=jax.ShapeDtypeStruct((n, D_MODEL), F32),
        compiler_params=_token_params(),
        name="ffn1",
    )(x2d, g, wg, wu, wd)


def _ffn2(h2d, sb2d, rw2d, wo_sb, wo_rw, g, wg, wu, wd, gf):
    n = h2d.shape[0]
    t = _token_tile(n)
    return pl.pallas_call(
        _ffn2_kernel,
        grid=(n // t,),
        in_specs=[_row_spec(t, D_MODEL), _row_spec(t, SB_WIDTH), _row_spec(t, RWKV_WIDTH),
                  _const_spec(wo_sb.shape), _const_spec(wo_rw.shape), _const_spec((1, D_MODEL)),
                  _const_spec(wg.shape), _const_spec(wu.shape), _const_spec(wd.shape),
                  _const_spec((1, D_MODEL))],
        out_specs=_row_spec(t, D_MODEL),
        out_shape=jax.ShapeDtypeStruct((n, D_MODEL), F32),
        compiler_params=_token_params(),
        name="ffn2",
    )(h2d, sb2d, rw2d, wo_sb, wo_rw, g, wg, wu, wd, gf)


def _in_proj(h2d, seq_len, g, wqkv, wrkv, wlora, mu_rkv, mu_lora, first_rkv, first_lora):
    n = h2d.shape[0]
    t = _token_tile(seq_len)
    n_rkv = 3 * RWKV_WIDTH
    last = lambda w: pl.BlockSpec((1, w), lambda i: (0, 0))
    return pl.pallas_call(
        functools.partial(_in_proj_kernel, tiles_per_seq=seq_len // t),
        grid=(n // t,),
        in_specs=[_row_spec(t, D_MODEL), _const_spec((1, D_MODEL)), _const_spec(wqkv.shape),
                  _const_spec(wrkv.shape), _const_spec(wlora.shape), _const_spec(mu_rkv.shape),
                  _const_spec(mu_lora.shape), _const_spec(first_rkv.shape),
                  _const_spec(first_lora.shape)],
        out_specs=[_row_spec(t, 3 * SB_WIDTH), _row_spec(t, n_rkv), _row_spec(t, LORA_PAD),
                   last(n_rkv), last(LORA_PAD)],
        out_shape=[jax.ShapeDtypeStruct((n, 3 * SB_WIDTH), BF16),
                   jax.ShapeDtypeStruct((n, n_rkv), F32),
                   jax.ShapeDtypeStruct((n, LORA_PAD), F32),
                   jax.ShapeDtypeStruct((1, n_rkv), F32),
                   jax.ShapeDtypeStruct((1, LORA_PAD), F32)],
        scratch_shapes=[pltpu.VMEM((1, n_rkv), F32), pltpu.VMEM((1, LORA_PAD), F32)],
        compiler_params=pltpu.CompilerParams(dimension_semantics=("arbitrary",),
                                             vmem_limit_bytes=VMEM_LIMIT),
        name="in_proj",
    )(h2d, g, wqkv, wrkv, wlora, mu_rkv, mu_lora, first_rkv, first_lora)


def _sb_attn_kernel(q_ref, k_ref, v_ref, km_ref, vm_ref, o_ref, qh_sc, acc_sc, rest_sc, *, tq, tk):
    i = pl.program_id(2)
    q = q_ref[0]
    low = lax.broadcasted_iota(jnp.int32, (1, LANES), 1) < HEAD_DIM
    zero = jnp.zeros_like(q)
    qh_sc[0] = jnp.where(low, q, zero)
    qh_sc[1] = jnp.where(low, zero, q)
    acc_sc[...] = jnp.zeros_like(acc_sc)
    rest_sc[...] = jnp.zeros_like(rest_sc)

    row = lax.broadcasted_iota(jnp.int32, (tk, tk), 0)
    col = lax.broadcasted_iota(jnp.int32, (tk, tk), 1)
    suffix_w = jnp.where(row > col, -1.0, 0.0).astype(BF16)
    diag_mask = col < row
    meta_mask = lax.broadcasted_iota(jnp.int32, (1, META_ROWS), 1) >= META_ROWS - N_META

    def sweep(chains, w, mask):
        if mask is not None:
            mask = jnp.concatenate([mask, mask], axis=0) if mask.shape[0] > 1 else mask
        z = [_dot_nt(jnp.concatenate([qh_sc[0, rows, :], qh_sc[1, rows, :]], axis=0), kb)
             for rows, kb, _, _ in chains]
        sp = [jnp.maximum(x, jnp.log2(1.0 + jnp.exp2(jnp.minimum(x, EXP2_CLAMP)))) for x in z]
        if mask is not None:
            sp = [jnp.where(mask, x, 0.0) for x in sp]
        sizes = [x.shape[0] for x in sp]
        cum = _dot(jnp.concatenate([x.astype(BF16) for x in sp], axis=0), w)
        cum = [cum[sum(sizes[:n]):sum(sizes[:n + 1])] for n in range(len(sp))]
        p = [jnp.exp2((a - b + c).astype(BF16)) for a, b, c in zip(z, sp, cum)]
        if mask is not None:
            p = [jnp.where(mask, x, jnp.zeros_like(x)) for x in p]
        pv = [_dot(x, c[2]) for x, c in zip(p, chains)]
        for (rows, _, _, valid), s, c, o in zip(chains, sp, cum, pv):
            n = o.shape[0] // 2
            for hh in range(2):
                half = slice(hh * n, (hh + 1) * n)
                rest = rest_sc[hh, rows, :]
                scale = jnp.exp2(rest)
                total = c[half, :1] - s[half, :1]
                if valid is not None:
                    scale = scale * valid
                    total = total * valid
                acc_sc[hh, rows, :] += scale * o[half]
                rest_sc[hh, rows, :] = rest + total

    n_blk = tq // tk

    def wave(s, mask):
        for rb0 in range(0, n_blk, SB_GROUP):
            chains = []
            for rb in range(rb0, rb0 + SB_GROUP):
                j = i * n_blk + rb - s
                valid = None if mask is not None else (j >= 0).astype(F32)
                start = pl.multiple_of(jnp.maximum(j, 0) * tk, tk)
                kb = k_ref[0, pl.ds(start, tk), :]
                vb = v_ref[0, pl.ds(start, tk), :]
                chains.append((slice(rb * tk, (rb + 1) * tk), kb, vb, valid))
            sweep(chains, suffix_w, mask)

    def stick_left(rb):
        rows = slice(rb * tk, (rb + 1) * tk)
        return jnp.maximum(jnp.max(rest_sc[0, rows, :]), jnp.max(rest_sc[1, rows, :]))

    def alive(s):
        left = [jnp.where(i * n_blk + rb - s >= 0, stick_left(rb), -jnp.inf)
                for rb in range(n_blk)]
        return functools.reduce(jnp.maximum, left) > DEAD_LOG2

    wave(0, diag_mask)
    wave(1, None)
    lax.while_loop(lambda c: c[1],
                   lambda c: (wave(c[0], None), (c[0] + 1, alive(c[0] + 1)))[1],
                   (jnp.int32(2), alive(2)))

    left = [stick_left(rb) for rb in range(n_blk)]
    for rb in range(n_blk):
        @pl.when(left[rb] > DEAD_LOG2)
        def _():
            sweep([(slice(rb * tk, (rb + 1) * tk), km_ref[...], vm_ref[...], None)],
                  suffix_w[tk - META_ROWS:, tk - META_ROWS:], meta_mask)

    o_ref[0] = jnp.where(low, acc_sc[0], acc_sc[1]).astype(o_ref.dtype)


def _sb_attention(qkv, qkv_meta):
    b, l, _ = qkv.shape
    tq = min(SB_Q_TILE, l)
    tk = min(SB_K_TILE, l)
    assert l % tq == 0 and tq % (tk * SB_GROUP) == 0 and tk >= META_ROWS
    n_pairs = SB_WIDTH // LANES
    return pl.pallas_call(
        functools.partial(_sb_attn_kernel, tq=tq, tk=tk),
        grid=(b, n_pairs, l // tq),
        in_specs=[
            pl.BlockSpec((1, tq, LANES), lambda bi, hp, i: (bi, i, hp)),
            pl.BlockSpec((1, l, LANES), lambda bi, hp, i: (bi, 0, n_pairs + hp)),
            pl.BlockSpec((1, l, LANES), lambda bi, hp, i: (bi, 0, 2 * n_pairs + hp)),
            pl.BlockSpec((META_ROWS, LANES), lambda bi, hp, i: (0, n_pairs + hp)),
            pl.BlockSpec((META_ROWS, LANES), lambda bi, hp, i: (0, 2 * n_pairs + hp)),
        ],
        out_specs=pl.BlockSpec((1, tq, LANES), lambda bi, hp, i: (bi, i, hp)),
        out_shape=jax.ShapeDtypeStruct((b, l, SB_WIDTH), BF16),
        scratch_shapes=[pltpu.VMEM((2, tq, LANES), BF16), pltpu.VMEM((2, tq, LANES), F32),
                        pltpu.VMEM((2, tq, 1), F32)],
        compiler_params=pltpu.CompilerParams(
            dimension_semantics=("parallel", "parallel", "arbitrary"),
            vmem_limit_bytes=VMEM_LIMIT),
        name="sb_attention",
    )(qkv, qkv, qkv, qkv_meta, qkv_meta)


def _rwkv_kernel(rkv_ref, lora_ref, h0_ref, w0_ref, wup_ref, a0_ref, aup_ref, gup_ref, kk_ref,
                 ka_ref, rk_ref, lnw_ref, lnb_ref, y_ref, hT_ref, h_sc, r_sc, k_sc, v_sc, kk_sc,
                 a_sc, lw_sc, y_sc, *, tile):
    c_w = RWKV_WIDTH

    @pl.when(pl.program_id(1) == 0)
    def _():
        h_sc[...] = h0_ref[...]

    rr = lax.broadcasted_iota(jnp.int32, (PAIR, PAIR), 0)
    cc = lax.broadcasted_iota(jnp.int32, (PAIR, PAIR), 1)
    same_head = (rr < HEAD_DIM) == (cc < HEAD_DIM)
    pair_sum = jnp.where(same_head, 1.0, 0.0).astype(BF16)

    def head_sums(x):
        x = x.astype(BF16)
        return jnp.concatenate([_dot(x[:, PAIR * pr:PAIR * (pr + 1)], pair_sum)
                                for pr in range(N_PAIRS)], axis=1)

    def prepare(r0, n):
        rows = slice(r0, r0 + n)
        p = rkv_ref[0, rows, :]
        lora = lora_ref[0, rows, :]
        r = p[:, :c_w]
        k = p[:, c_w:2 * c_w]
        w = -jax.nn.softplus(-(w0_ref[...] + _bdot(jnp.tanh(lora), wup_ref[...]))) - 0.5
        a = jax.nn.sigmoid(a0_ref[...] + _bdot(lora, aup_ref[...]))
        kk = k * kk_ref[...]
        r_sc[rows, :] = r
        k_sc[rows, :] = k * (1.0 + (a - 1.0) * ka_ref[...])
        v_sc[rows, :] = p[:, 2 * c_w:]
        kk_sc[rows, :] = kk * lax.rsqrt(jnp.maximum(head_sums(kk * kk), 1e-24))
        a_sc[rows, :] = a
        lw_sc[rows, :] = -jnp.exp(w)
        return _bdot(jax.nn.sigmoid(lora), gup_ref[...])

    strict_lower = same_head & (cc < rr)
    lower = same_head & (cc <= rr)
    eye_mask = rr == cc
    eye = jnp.where(eye_mask, 1.0, 0.0)
    tr = lax.broadcasted_iota(jnp.int32, (CHUNK, CHUNK), 0)
    tc = lax.broadcasted_iota(jnp.int32, (CHUNK, CHUNK), 1)
    cumsum_w = jnp.where(tc <= tr, 1.0, 0.0).astype(BF16)
    zero = jnp.zeros((PAIR, PAIR), BF16)

    def chunk_operands(c):
        rows = slice(c * CHUNK, (c + 1) * CHUNK)
        lw = lw_sc[rows, :]
        lw_hi = lw.astype(BF16)
        g = _dot(cumsum_w, lw_hi) + _dot(cumsum_w, (lw - lw_hi.astype(F32)).astype(BF16))
        g_end = g[CHUNK - 1:CHUNK, :]
        decay_out = jnp.exp(-g)
        decay_end = jnp.exp(g_end - g)
        kk_c = kk_sc[rows, :]
        kka = kk_c * a_sc[rows, :]
        k_c = k_sc[rows, :]
        full = dict(
            a_bar=(-kk_c * jnp.exp(g - lw)).astype(BF16),
            r_bar=(r_sc[rows, :] * jnp.exp(g)).astype(BF16),
            b_til=(kka * decay_out).astype(BF16), k_til=(k_c * decay_out).astype(BF16),
            b_hat=(kka * decay_end).astype(BF16), k_hat=(k_c * decay_end).astype(BF16),
            v=v_sc[rows, :].astype(BF16))
        gamma_end = jnp.exp(g_end)
        masked = ("a_bar", "r_bar", "v", "b_hat", "k_hat")
        out = []
        for pr in range(N_PAIRS):
            sl = slice(PAIR * pr, PAIR * (pr + 1))
            ops = {}
            for name, x in full.items():
                x2 = jnp.concatenate([x[:, sl], x[:, sl]], axis=0)
                ops[name] = jnp.where(same_head, x2, zero) if name in masked else x2
            ops.update(rows=rows, sl=sl, pr=pr, gamma_end=gamma_end[:, sl],
                       r_pair=full["r_bar"][:, sl])
            out.append(ops)
        return out

    def state_free(chunks, out):
        chains = [ops for c in chunks for ops in chunk_operands(c)]
        s1 = [_dot_nt(jnp.concatenate([o["a_bar"], o["r_bar"]], axis=0),
                      jnp.concatenate([o["b_til"], o["k_til"]], axis=0)) for o in chains]
        n_mat = [jnp.where(strict_lower, s[:PAIR, :PAIR], 0.0) for s in s1]
        a_ak = [jnp.where(strict_lower, s[:PAIR, PAIR:], 0.0).astype(BF16) for s in s1]
        a_rb = [jnp.where(lower, s[PAIR:, :PAIR], 0.0).astype(BF16) for s in s1]
        a_rk = [jnp.where(lower, s[PAIR:, PAIR:], 0.0).astype(BF16) for s in s1]
        yield
        inv = [eye + n for n in n_mat]
        power = [n.astype(BF16) for n in n_mat]
        for _ in range(5):
            power = [_dot(p, p).astype(BF16) for p in power]
            yield
            inv = [i + _dot(i.astype(BF16), p) for i, p in zip(inv, power)]
            yield
        akv = [_dot(a, o["v"]).astype(BF16) for a, o in zip(a_ak, chains)]
        yield
        pq = [_dot(i.astype(BF16), jnp.concatenate([o["a_bar"], x], axis=1)).astype(BF16)
              for i, o, x in zip(inv, chains, akv)]
        yield
        pqv = [jnp.concatenate([x, jnp.concatenate([zero, o["v"]], axis=1)], axis=0)
               for x, o in zip(pq, chains)]
        ef = [_dot(jnp.concatenate([b, k], axis=1), x) for b, k, x in zip(a_rb, a_rk, pqv)]
        ef = [x[:CHUNK] + x[CHUNK:] for x in ef]
        yield
        mg = [_dot_tn(jnp.concatenate([o["b_hat"], o["k_hat"]], axis=0), x)
              for o, x in zip(chains, pqv)]
        out.extend(zip(chains, ef, mg))

    def advance_state(items):
        for o, x, m in items:
            e_mat = x[:, :PAIR] + o["r_pair"].astype(F32)
            m_mat = m[:, :PAIR] + jnp.where(eye_mask, o["gamma_end"], 0.0)
            eh = _bdot(jnp.concatenate([e_mat, m_mat], axis=0), h_sc[o["pr"]])
            y_sc[o["rows"], o["sl"]] = eh[:CHUNK] + x[:, PAIR:]
            h_sc[o["pr"]] = eh[CHUNK:] + m[:, PAIR:]

    def finish(r0, n, gate):
        rows = slice(r0, r0 + n)
        y = y_sc[rows, :]
        inv_n = 1.0 / HEAD_DIM
        d = y - head_sums(y) * inv_n
        yn = d * lax.rsqrt(head_sums(d * d) * inv_n + LNX_EPS)
        yn = yn * lnw_ref[...] + lnb_ref[...]
        bonus = head_sums(r_sc[rows, :] * k_sc[rows, :] * rk_ref[...]) * v_sc[rows, :]
        y_ref[0, rows, :] = ((yn + bonus) * gate).astype(y_ref.dtype)

    n_chunks = tile // CHUNK
    first = list(range((n_chunks + 1) // 2))
    second = list(range(len(first), n_chunks))
    n_first = len(first) * CHUNK
    gate_a = prepare(0, n_first)
    done_a, done_b = [], []
    stages_a = state_free(first, done_a)
    next(stages_a)
    gate_b = prepare(n_first, tile - n_first) if second else None
    for _ in stages_a:
        pass
    pending = [functools.partial(advance_state, done_a[c * N_PAIRS:(c + 1) * N_PAIRS])
               for c in range(len(first))] + [functools.partial(finish, 0, n_first, gate_a)]
    for _ in state_free(second, done_b) if second else ():
        if pending:
            pending.pop(0)()
    while pending:
        pending.pop(0)()
    for c in range(len(second)):
        advance_state(done_b[c * N_PAIRS:(c + 1) * N_PAIRS])
    if second:
        finish(n_first, tile - n_first, gate_b)
    hT_ref[0] = h_sc[...]


def _rwkv(rkv, lora, h0, params):
    b, l, _ = rkv.shape
    tile = min(RWKV_TILE, l)
    assert l % tile == 0 and tile % CHUNK == 0
    seq = lambda w: pl.BlockSpec((1, tile, w), lambda bi, t: (bi, t, 0))
    scr = lambda: pltpu.VMEM((tile, RWKV_WIDTH), F32)
    return pl.pallas_call(
        functools.partial(_rwkv_kernel, tile=tile),
        grid=(b, l // tile),
        in_specs=[seq(3 * RWKV_WIDTH), seq(LORA_PAD), _const_spec(h0.shape)]
                 + [_const_spec(p.shape) for p in params],
        out_specs=[seq(RWKV_WIDTH),
                   pl.BlockSpec((1, N_PAIRS, PAIR, PAIR), lambda bi, t: (bi, 0, 0, 0))],
        out_shape=[jax.ShapeDtypeStruct((b, l, RWKV_WIDTH), BF16),
                   jax.ShapeDtypeStruct((b, N_PAIRS, PAIR, PAIR), F32)],
        scratch_shapes=[pltpu.VMEM((N_PAIRS, PAIR, PAIR), F32),
                        scr(), scr(), scr(), scr(), scr(), scr(), scr()],
        compiler_params=pltpu.CompilerParams(dimension_semantics=("parallel", "arbitrary"),
                                             vmem_limit_bytes=VMEM_LIMIT),
        name="rwkv7",
    )(rkv, lora, h0, *params)


def _pad_rows(w, start, total):
    return jnp.pad(w, ((start, total - start - w.shape[0]), (0, 0)))


def kernel(x, meta_tokens, ffn1_norm, ffn1_w_gate, ffn1_w_up, ffn1_w_down, mix_norm, w_in, rwkv_mu, rwkv_w0, rwkv_w_up, rwkv_a0, rwkv_a_up, rwkv_g_up, rwkv_k_k, rwkv_k_a, rwkv_r_k, rwkv_lnx_w, rwkv_lnx_b, w_out, ffn2_norm, ffn2_w_gate, ffn2_w_up, ffn2_w_down, final_norm):
    b, l, _ = x.shape
    depth = ffn1_norm.shape[0]
    assert depth == 1, "stacked layers would need the meta rows carried through the mixers"
    row = lambda p: p.reshape(1, -1).astype(F32)
    n_rkv = 3 * RWKV_WIDTH
    n_lora = W_LORA + A_LORA + G_LORA

    h = x.reshape(b * l, D_MODEL)
    h_meta = _pad_rows(meta_tokens.astype(F32), META_ROWS - N_META, META_ROWS)
    for d in range(depth):
        w_qkv = w_in[d][:, :3 * SB_WIDTH]
        w_qkv = w_qkv.at[:, :SB_WIDTH].multiply(HEAD_DIM ** -0.5 * LOG2_E).astype(BF16)
        w_rkv = w_in[d][:, 3 * SB_WIDTH:3 * SB_WIDTH + n_rkv].astype(BF16)
        w_lora = jnp.pad(w_in[d][:, 3 * SB_WIDTH + n_rkv:],
                         ((0, 0), (0, LORA_PAD - n_lora))).astype(BF16)
        mu_rkv = row(rwkv_mu[d][:n_rkv])
        mu_lora = jnp.pad(row(rwkv_mu[d][n_rkv:]), ((0, 0), (0, LORA_PAD - n_lora)))
        rwkv_params = (
            row(rwkv_w0[d]), _pad_rows(rwkv_w_up[d], 0, LORA_PAD),
            row(rwkv_a0[d]), _pad_rows(rwkv_a_up[d], W_LORA, LORA_PAD),
            _pad_rows(rwkv_g_up[d], W_LORA + A_LORA, LORA_PAD), row(rwkv_k_k[d]),
            row(rwkv_k_a[d]), row(rwkv_r_k[d]), row(rwkv_lnx_w[d]), row(rwkv_lnx_b[d]))
        ffn1_w = (row(ffn1_norm[d]), ffn1_w_gate[d].astype(BF16), ffn1_w_up[d].astype(BF16),
                  ffn1_w_down[d].astype(BF16))
        ffn2_w = (row(ffn2_norm[d]), ffn2_w_gate[d].astype(BF16), ffn2_w_up[d].astype(BF16),
                  ffn2_w_down[d].astype(BF16))
        wo_sb = w_out[d][:SB_WIDTH].astype(BF16)
        wo_rw = w_out[d][SB_WIDTH:].astype(BF16)
        gf = row(final_norm)

        h = _ffn1(h, *ffn1_w)
        h_meta = _ffn1(h_meta, *ffn1_w)
        in_w = (row(mix_norm[d]), w_qkv, w_rkv, w_lora, mu_rkv, mu_lora)
        qkv_m, rkv_m, lora_m, last_rkv_m, last_lora_m = _in_proj(
            h_meta, META_ROWS, *in_w, jnp.zeros((1, n_rkv), F32), jnp.zeros((1, LORA_PAD), F32))
        qkv, rkv, lora, _, _ = _in_proj(h, l, *in_w, last_rkv_m, last_lora_m)

        sb = _sb_attention(qkv.reshape(b, l, -1), qkv_m)

        zero_state = jnp.zeros((N_PAIRS, PAIR, PAIR), F32)
        _, h_meta_state = _rwkv(rkv_m[None], lora_m[None], zero_state, rwkv_params)
        rw, _ = _rwkv(rkv.reshape(b, l, -1), lora.reshape(b, l, -1), h_meta_state[0],
                      rwkv_params)

        h = _ffn2(h, sb.reshape(b * l, -1), rw.reshape(b * l, -1), wo_sb, wo_rw, *ffn2_w, gf)
    return h.reshape(b, l, D_MODEL)
```

```python
import functools

import jax
import jax.numpy as jnp
from jax import lax
from jax.experimental import pallas as pl
from jax.experimental.pallas import tpu as pltpu

D_MODEL = 1024
N_META = 16
HEAD_DIM = 64
SB_WIDTH = 512
RWKV_WIDTH = 512
D_FF = 2816
W_LORA = 32
A_LORA = 32
G_LORA = 96
RMS_EPS = 1e-6
LNX_EPS = 64e-5

LANES = 128
META_ROWS = 128
LORA_PAD = 256
CHUNK = 64
PAIR = 2 * HEAD_DIM
N_PAIRS = RWKV_WIDTH // PAIR
VMEM_LIMIT = 56 * 1024 * 1024
TOKEN_TILE = 512
RWKV_TILE = 512
SB_Q_TILE = 4096
SB_K_TILE = 256
SB_GROUP = 2

F32 = jnp.float32
BF16 = jnp.bfloat16
LOG2_E = 1.4426950408889634
EXP2_CLAMP = 126.0
DEAD_LOG2 = -150.0


def _const_spec(shape):
    zeros = (0,) * len(shape)
    return pl.BlockSpec(shape, lambda *_: zeros, pipeline_mode=pl.Buffered(1))


def _rms(x, g):
    ms = jnp.mean(x * x, axis=-1, keepdims=True)
    return x * lax.rsqrt(ms + RMS_EPS) * g


def _dot(a, b):
    return jnp.dot(a, b, preferred_element_type=F32)


def _bdot(a, b):
    return _dot(a.astype(BF16), b.astype(BF16))


def _dot_nt(a, b):
    return lax.dot_general(a, b, (((1,), (1,)), ((), ())), preferred_element_type=F32)


def _dot_tn(a, b):
    return lax.dot_general(a, b, (((0,), (0,)), ((), ())), preferred_element_type=F32)


def _swiglu_half_step(h, g, wg_ref, wu_ref, wd_ref):
    n = _rms(h, g).astype(BF16)
    gate = _dot(n, wg_ref[...])
    up = _dot(n, wu_ref[...])
    act = (gate * jax.nn.sigmoid(gate) * up).astype(BF16)
    return h + 0.5 * _dot(act, wd_ref[...])


def _ffn1_kernel(x_ref, g_ref, wg_ref, wu_ref, wd_ref, o_ref):
    o_ref[...] = _swiglu_half_step(x_ref[...], g_ref[...], wg_ref, wu_ref, wd_ref)


def _ffn2_kernel(h_ref, sb_ref, rw_ref, wo_sb_ref, wo_rw_ref, g_ref, wg_ref, wu_ref, wd_ref,
                 gf_ref, o_ref):
    h = h_ref[...] + _dot(sb_ref[...], wo_sb_ref[...]) + _dot(rw_ref[...], wo_rw_ref[...])
    h = _swiglu_half_step(h, g_ref[...], wg_ref, wu_ref, wd_ref)
    o_ref[...] = _rms(h, gf_ref[...])


def _in_proj_kernel(h_ref, g_ref, wqkv_ref, wrkv_ref, wlora_ref, mu_rkv_ref, mu_lora_ref,
                    first_rkv_ref, first_lora_ref, qkv_ref, rkv_ref, lora_ref, last_rkv_ref,
                    last_lora_ref, prev_rkv_sc, prev_lora_sc, *, tiles_per_seq):
    tile = h_ref.shape[0]

    @pl.when(pl.program_id(0) % tiles_per_seq == 0)
    def _():
        prev_rkv_sc[...] = first_rkv_ref[...]
        prev_lora_sc[...] = first_lora_ref[...]

    n = _rms(h_ref[...], g_ref[...]).astype(BF16)
    qkv_ref[...] = _dot(n, wqkv_ref[...]).astype(BF16)
    first_row = lax.broadcasted_iota(jnp.int32, (tile, 1), 0) == 0

    def shifted(w_ref, mu_ref, prev_sc, out_ref, last_ref):
        x = _dot(n, w_ref[...])
        x_prev = jnp.where(first_row, prev_sc[...], pltpu.roll(x, 1, axis=0))
        out_ref[...] = x + (x_prev - x) * mu_ref[...]
        prev_sc[...] = x[tile - 1:tile, :]
        last_ref[...] = x[tile - 1:tile, :]

    shifted(wrkv_ref, mu_rkv_ref, prev_rkv_sc, rkv_ref, last_rkv_ref)
    shifted(wlora_ref, mu_lora_ref, prev_lora_sc, lora_ref, last_lora_ref)


def _token_tile(n_tokens):
    t = min(TOKEN_TILE, n_tokens)
    assert n_tokens % t == 0 and t % 8 == 0, (n_tokens, t)
    return t


def _row_spec(t, width):
    return pl.BlockSpec((t, width), lambda i: (i, 0))


def _token_params():
    return pltpu.CompilerParams(dimension_semantics=("parallel",), vmem_limit_bytes=VMEM_LIMIT)


def _ffn1(x2d, g, wg, wu, wd):
    n = x2d.shape[0]
    t = _token_tile(n)
    return pl.pallas_call(
        _ffn1_kernel,
        grid=(n // t,),
        in_specs=[_row_spec(t, D_MODEL), _const_spec((1, D_MODEL)), _const_spec(wg.shape),
                  _const_spec(wu.shape), _const_spec(wd.shape)],
        out_specs=_row_spec(t, D_MODEL),
        out_shape=jax.ShapeDtypeStruct((n, D_MODEL), F32),
        compiler_params=_token_params(),
        name="ffn1",
    )(x2d, g, wg, wu, wd)


def _ffn2(h2d, sb2d, rw2d, wo_sb, wo_rw, g, wg, wu, wd, gf):
    n = h2d.shape[0]
    t = _token_tile(n)
    return pl.pallas_call(
        _ffn2_kernel,
        grid=(n // t,),
        in_specs=[_row_spec(t, D_MODEL), _row_spec(t, SB_WIDTH), _row_spec(t, RWKV_WIDTH),
                  _const_spec(wo_sb.shape), _const_spec(wo_rw.shape), _const_spec((1, D_MODEL)),
                  _const_spec(wg.shape), _const_spec(wu.shape), _const_spec(wd.shape),
                  _const_spec((1, D_MODEL))],
        out_specs=_row_spec(t, D_MODEL),
        out_shape=jax.ShapeDtypeStruct((n, D_MODEL), F32),
        compiler_params=_token_params(),
        name="ffn2",
    )(h2d, sb2d, rw2d, wo_sb, wo_rw, g, wg, wu, wd, gf)


def _in_proj(h2d, seq_len, g, wqkv, wrkv, wlora, mu_rkv, mu_lora, first_rkv, first_lora):
    n = h2d.shape[0]
    t = _token_tile(seq_len)
    n_rkv = 3 * RWKV_WIDTH
    last = lambda w: pl.BlockSpec((1, w), lambda i: (0, 0))
    return pl.pallas_call(
        functools.partial(_in_proj_kernel, tiles_per_seq=seq_len // t),
        grid=(n // t,),
        in_specs=[_row_spec(t, D_MODEL), _const_spec((1, D_MODEL)), _const_spec(wqkv.shape),
                  _const_spec(wrkv.shape), _const_spec(wlora.shape), _const_spec(mu_rkv.shape),
                  _const_spec(mu_lora.shape), _const_spec(first_rkv.shape),
                  _const_spec(first_lora.shape)],
        out_specs=[_row_spec(t, 3 * SB_WIDTH), _row_spec(t, n_rkv), _row_spec(t, LORA_PAD),
                   last(n_rkv), last(LORA_PAD)],
        out_shape=[jax.ShapeDtypeStruct((n, 3 * SB_WIDTH), BF16),
                   jax.ShapeDtypeStruct((n, n_rkv), F32),
                   jax.ShapeDtypeStruct((n, LORA_PAD), F32),
                   jax.ShapeDtypeStruct((1, n_rkv), F32),
                   jax.ShapeDtypeStruct((1, LORA_PAD), F32)],
        scratch_shapes=[pltpu.VMEM((1, n_rkv), F32), pltpu.VMEM((1, LORA_PAD), F32)],
        compiler_params=pltpu.CompilerParams(dimension_semantics=("arbitrary",),
                                             vmem_limit_bytes=VMEM_LIMIT),
        name="in_proj",
    )(h2d, g, wqkv, wrkv, wlora, mu_rkv, mu_lora, first_rkv, first_lora)


def _sb_attn_kernel(q_ref, k_ref, v_ref, km_ref, vm_ref, o_ref, qh_sc, acc_sc, rest_sc, *, tq, tk,
                    single_tile):
    i = 0 if single_tile else pl.program_id(2)
    q = q_ref[0]
    low = lax.broadcasted_iota(jnp.int32, (1, LANES), 1) < HEAD_DIM
    zero = jnp.zeros_like(q)
    qh_sc[0] = jnp.where(low, q, zero)
    qh_sc[1] = jnp.where(low, zero, q)
    acc_sc[...] = jnp.zeros_like(acc_sc)
    rest_sc[...] = jnp.zeros_like(rest_sc)

    row = lax.broadcasted_iota(jnp.int32, (tk, tk), 0)
    col = lax.broadcasted_iota(jnp.int32, (tk, tk), 1)
    suffix_w = jnp.where(row > col, -1.0, 0.0).astype(BF16)
    diag_mask = col < row
    meta_mask = lax.broadcasted_iota(jnp.int32, (1, META_ROWS), 1) >= META_ROWS - N_META

    def sweep(chains, w, mask):
        if mask is not None:
            mask = jnp.concatenate([mask, mask], axis=0) if mask.shape[0] > 1 else mask
        z = [_dot_nt(jnp.concatenate([qh_sc[0, rows, :], qh_sc[1, rows, :]], axis=0), kb)
             for rows, kb, _, _ in chains]
        sp = [jnp.maximum(x, jnp.log2(1.0 + jnp.exp2(jnp.minimum(x, EXP2_CLAMP)))) for x in z]
        if mask is not None:
            sp = [jnp.where(mask, x, 0.0) for x in sp]
        sizes = [x.shape[0] for x in sp]
        cum = _dot(jnp.concatenate([x.astype(BF16) for x in sp], axis=0), w)
        cum = [cum[sum(sizes[:n]):sum(sizes[:n + 1])] for n in range(len(sp))]
        p = [jnp.exp2((a - b + c).astype(BF16)) for a, b, c in zip(z, sp, cum)]
        if mask is not None:
            p = [jnp.where(mask, x, jnp.zeros_like(x)) for x in p]
        pv = [_dot(x, c[2]) for x, c in zip(p, chains)]
        for (rows, _, _, valid), s, c, o in zip(chains, sp, cum, pv):
            n = o.shape[0] // 2
            for hh in range(2):
                half = slice(hh * n, (hh + 1) * n)
                rest = rest_sc[hh, rows, :]
                scale = jnp.exp2(rest)
                total = c[half, :1] - s[half, :1]
                if valid is not None:
                    scale = scale * valid
                    total = total * valid
                acc_sc[hh, rows, :] += scale * o[half]
                rest_sc[hh, rows, :] = rest + total

    n_blk = tq // tk

    def wave(s, mask):
        for rb0 in range(0, n_blk, SB_GROUP):
            chains = []
            for rb in range(rb0, rb0 + SB_GROUP):
                j = i * n_blk + rb - s
                if isinstance(j, int):
                    if j < 0:
                        continue
                    valid, start = None, j * tk
                else:
                    valid = None if mask is not None else (j >= 0).astype(F32)
                    start = pl.multiple_of(jnp.maximum(j, 0) * tk, tk)
                kb = k_ref[0, pl.ds(start, tk), :]
                vb = v_ref[0, pl.ds(start, tk), :]
                chains.append((slice(rb * tk, (rb + 1) * tk), kb, vb, valid))
            sweep(chains, suffix_w, mask)

    def stick_left(rb):
        rows = slice(rb * tk, (rb + 1) * tk)
        return jnp.maximum(jnp.max(rest_sc[0, rows, :]), jnp.max(rest_sc[1, rows, :]))

    def alive(s, left=None):
        left = [jnp.where(i * n_blk + rb - s >= 0,
                          stick_left(rb) if left is None else left[rb], -jnp.inf)
                for rb in range(n_blk)]
        return functools.reduce(jnp.maximum, left) > DEAD_LOG2

    wave(0, diag_mask)
    wave(1, None)
    left = [stick_left(rb) for rb in range(n_blk)]
    lax.while_loop(lambda c: c[1],
                   lambda c: (wave(c[0], None), (c[0] + 1, alive(c[0] + 1)))[1],
                   (jnp.int32(2), alive(2, left)))

    for rb in range(n_blk):
        @pl.when(left[rb] > DEAD_LOG2)
        def _():
            sweep([(slice(rb * tk, (rb + 1) * tk), km_ref[...], vm_ref[...], None)],
                  suffix_w[tk - META_ROWS:, tk - META_ROWS:], meta_mask)

    o_ref[0] = jnp.where(low, acc_sc[0], acc_sc[1]).astype(o_ref.dtype)


def _sb_attention(qkv, qkv_meta):
    b, l, _ = qkv.shape
    tq = min(SB_Q_TILE, l)
    tk = min(SB_K_TILE, l)
    assert l % tq == 0 and tq % (tk * SB_GROUP) == 0 and tk >= META_ROWS
    n_pairs = SB_WIDTH // LANES
    return pl.pallas_call(
        functools.partial(_sb_attn_kernel, tq=tq, tk=tk, single_tile=l == tq),
        grid=(b, n_pairs, l // tq),
        in_specs=[
            pl.BlockSpec((1, tq, LANES), lambda bi, hp, i: (bi, i, hp)),
            pl.BlockSpec((1, l, LANES), lambda bi, hp, i: (bi, 0, n_pairs + hp)),
            pl.BlockSpec((1, l, LANES), lambda bi, hp, i: (bi, 0, 2 * n_pairs + hp)),
            pl.BlockSpec((META_ROWS, LANES), lambda bi, hp, i: (0, n_pairs + hp)),
            pl.BlockSpec((META_ROWS, LANES), lambda bi, hp, i: (0, 2 * n_pairs + hp)),
        ],
        out_specs=pl.BlockSpec((1, tq, LANES), lambda bi, hp, i: (bi, i, hp)),
        out_shape=jax.ShapeDtypeStruct((b, l, SB_WIDTH), BF16),
        scratch_shapes=[pltpu.VMEM((2, tq, LANES), BF16), pltpu.VMEM((2, tq, LANES), F32),
                        pltpu.VMEM((2, tq, 1), F32)],
        compiler_params=pltpu.CompilerParams(
            dimension_semantics=("parallel", "parallel", "arbitrary"),
            vmem_limit_bytes=VMEM_LIMIT),
        name="sb_attention",
    )(qkv, qkv, qkv, qkv_meta, qkv_meta)


def _rwkv_kernel(rkv_ref, lora_ref, h0_ref, w0_ref, wup_ref, a0_ref, aup_ref, gup_ref, kk_ref,
                 ka_ref, rk_ref, lnw_ref, lnb_ref, y_ref, hT_ref, h_sc, r_sc, k_sc, v_sc, kk_sc,
                 a_sc, lw_sc, y_sc, *, tile):
    c_w = RWKV_WIDTH

    @pl.when(pl.program_id(1) == 0)
    def _():
        h_sc[...] = h0_ref[...]

    rr = lax.broadcasted_iota(jnp.int32, (PAIR, PAIR), 0)
    cc = lax.broadcasted_iota(jnp.int32, (PAIR, PAIR), 1)
    same_head = (rr < HEAD_DIM) == (cc < HEAD_DIM)
    pair_sum = jnp.where(same_head, 1.0, 0.0).astype(BF16)

    def head_sums(x):
        x = x.astype(BF16)
        return jnp.concatenate([_dot(x[:, PAIR * pr:PAIR * (pr + 1)], pair_sum)
                                for pr in range(N_PAIRS)], axis=1)

    def prepare(r0, n):
        rows = slice(r0, r0 + n)
        p = rkv_ref[0, rows, :]
        lora = lora_ref[0, rows, :]
        r = p[:, :c_w]
        k = p[:, c_w:2 * c_w]
        w = -jax.nn.softplus(-(w0_ref[...] + _bdot(jnp.tanh(lora), wup_ref[...]))) - 0.5
        a = jax.nn.sigmoid(a0_ref[...] + _bdot(lora, aup_ref[...]))
        kk = k * kk_ref[...]
        r_sc[rows, :] = r
        k_sc[rows, :] = k * (1.0 + (a - 1.0) * ka_ref[...])
        v_sc[rows, :] = p[:, 2 * c_w:]
        kk_sc[rows, :] = kk * lax.rsqrt(jnp.maximum(head_sums(kk * kk), 1e-24))
        a_sc[rows, :] = a
        lw_sc[rows, :] = -jnp.exp(w)
        return _bdot(jax.nn.sigmoid(lora), gup_ref[...])

    strict_lower = same_head & (cc < rr)
    lower = same_head & (cc <= rr)
    eye_mask = rr == cc
    eye = jnp.where(eye_mask, 1.0, 0.0)
    tr = lax.broadcasted_iota(jnp.int32, (CHUNK, CHUNK), 0)
    tc = lax.broadcasted_iota(jnp.int32, (CHUNK, CHUNK), 1)
    cumsum_w = jnp.where(tc <= tr, 1.0, 0.0).astype(BF16)
    zero = jnp.zeros((PAIR, PAIR), BF16)

    def chunk_operands(c):
        rows = slice(c * CHUNK, (c + 1) * CHUNK)
        lw = lw_sc[rows, :]
        lw_hi = lw.astype(BF16)
        g = _dot(cumsum_w, lw_hi) + _dot(cumsum_w, (lw - lw_hi.astype(F32)).astype(BF16))
        g_end = g[CHUNK - 1:CHUNK, :]
        decay_out = jnp.exp(-g)
        decay_end = jnp.exp(g_end - g)
        kk_c = kk_sc[rows, :]
        kka = kk_c * a_sc[rows, :]
        k_c = k_sc[rows, :]
        full = dict(
            a_bar=(-kk_c * jnp.exp(g - lw)).astype(BF16),
            r_bar=(r_sc[rows, :] * jnp.exp(g)).astype(BF16),
            b_til=(kka * decay_out).astype(BF16), k_til=(k_c * decay_out).astype(BF16),
            b_hat=(kka * decay_end).astype(BF16), k_hat=(k_c * decay_end).astype(BF16),
            v=v_sc[rows, :].astype(BF16))
        gamma_end = jnp.exp(g_end)
        masked = ("a_bar", "r_bar", "v", "b_hat", "k_hat")
        out = []
        for pr in range(N_PAIRS):
            sl = slice(PAIR * pr, PAIR * (pr + 1))
            ops = {}
            for name, x in full.items():
                x2 = jnp.concatenate([x[:, sl], x[:, sl]], axis=0)
                ops[name] = jnp.where(same_head, x2, zero) if name in masked else x2
            ops.update(rows=rows, sl=sl, pr=pr, gamma_end=gamma_end[:, sl],
                       r_pair=full["r_bar"][:, sl])
            out.append(ops)
        return out

    def state_free(chunks, out):
        chains = [ops for c in chunks for ops in chunk_operands(c)]
        s1 = [_dot_nt(jnp.concatenate([o["a_bar"], o["r_bar"]], axis=0),
                      jnp.concatenate([o["b_til"], o["k_til"]], axis=0)) for o in chains]
        n_mat = [jnp.where(strict_lower, s[:PAIR, :PAIR], 0.0) for s in s1]
        a_ak = [jnp.where(strict_lower, s[:PAIR, PAIR:], 0.0).astype(BF16) for s in s1]
        a_rb = [jnp.where(lower, s[PAIR:, :PAIR], 0.0).astype(BF16) for s in s1]
        a_rk = [jnp.where(lower, s[PAIR:, PAIR:], 0.0).astype(BF16) for s in s1]
        yield
        inv = [eye + n for n in n_mat]
        power = [n.astype(BF16) for n in n_mat]
        for _ in range(5):
            power = [_dot(p, p).astype(BF16) for p in power]
            yield
            inv = [i + _dot(i.astype(BF16), p) for i, p in zip(inv, power)]
            yield
        akv = [_dot(a, o["v"]).astype(BF16) for a, o in zip(a_ak, chains)]
        yield
        pq = [_dot(i.astype(BF16), jnp.concatenate([o["a_bar"], x], axis=1)).astype(BF16)
              for i, o, x in zip(inv, chains, akv)]
        yield
        pqv = [jnp.concatenate([x, jnp.concatenate([zero, o["v"]], axis=1)], axis=0)
               for x, o in zip(pq, chains)]
        ef = [_dot(jnp.concatenate([b, k], axis=1), x) for b, k, x in zip(a_rb, a_rk, pqv)]
        ef = [x[:CHUNK] + x[CHUNK:] for x in ef]
        yield
        mg = [_dot_tn(jnp.concatenate([o["b_hat"], o["k_hat"]], axis=0), x)
              for o, x in zip(chains, pqv)]
        out.extend(zip(chains, ef, mg))

    def advance_state(items):
        for o, x, m in items:
            e_mat = x[:, :PAIR] + o["r_pair"].astype(F32)
            m_mat = m[:, :PAIR] + jnp.where(eye_mask, o["gamma_end"], 0.0)
            eh = _bdot(jnp.concatenate([e_mat, m_mat], axis=0), h_sc[o["pr"]])
            y_sc[o["rows"], o["sl"]] = eh[:CHUNK] + x[:, PAIR:]
            h_sc[o["pr"]] = eh[CHUNK:] + m[:, PAIR:]

    def finish(r0, n, gate):
        rows = slice(r0, r0 + n)
        y = y_sc[rows, :]
        inv_n = 1.0 / HEAD_DIM
        d = y - head_sums(y) * inv_n
        yn = d * lax.rsqrt(head_sums(d * d) * inv_n + LNX_EPS)
        yn = yn * lnw_ref[...] + lnb_ref[...]
        bonus = head_sums(r_sc[rows, :] * k_sc[rows, :] * rk_ref[...]) * v_sc[rows, :]
        y_ref[0, rows, :] = ((yn + bonus) * gate).astype(y_ref.dtype)

    n_chunks = tile // CHUNK
    first = list(range((n_chunks + 1) // 2))
    second = list(range(len(first), n_chunks))
    n_first = len(first) * CHUNK
    gate_a = prepare(0, n_first)
    done_a, done_b = [], []
    stages_a = state_free(first, done_a)
    next(stages_a)
    gate_b = prepare(n_first, tile - n_first) if second else None
    for _ in stages_a:
        pass
    pending = [functools.partial(advance_state, done_a[c * N_PAIRS:(c + 1) * N_PAIRS])
               for c in range(len(first))] + [functools.partial(finish, 0, n_first, gate_a)]
    for _ in state_free(second, done_b) if second else ():
        if pending:
            pending.pop(0)()
    while pending:
        pending.pop(0)()
    for c in range(len(second)):
        advance_state(done_b[c * N_PAIRS:(c + 1) * N_PAIRS])
    if second:
        finish(n_first, tile - n_first, gate_b)
    hT_ref[0] = h_sc[...]


def _rwkv(rkv, lora, h0, params):
    b, l, _ = rkv.shape
    tile = min(RWKV_TILE, l)
    assert l % tile == 0 and tile % CHUNK == 0
    seq = lambda w: pl.BlockSpec((1, tile, w), lambda bi, t: (bi, t, 0))
    scr = lambda: pltpu.VMEM((tile, RWKV_WIDTH), F32)
    return pl.pallas_call(
        functools.partial(_rwkv_kernel, tile=tile),
        grid=(b, l // tile),
        in_specs=[seq(3 * RWKV_WIDTH), seq(LORA_PAD), _const_spec(h0.shape)]
                 + [_const_spec(p.shape) for p in params],
        out_specs=[seq(RWKV_WIDTH),
                   pl.BlockSpec((1, N_PAIRS, PAIR, PAIR), lambda bi, t: (bi, 0, 0, 0))],
        out_shape=[jax.ShapeDtypeStruct((b, l, RWKV_WIDTH), BF16),
                   jax.ShapeDtypeStruct((b, N_PAIRS, PAIR, PAIR), F32)],
        scratch_shapes=[pltpu.VMEM((N_PAIRS, PAIR, PAIR), F32),
                        scr(), scr(), scr(), scr(), scr(), scr(), scr()],
        compiler_params=pltpu.CompilerParams(dimension_semantics=("parallel", "arbitrary"),
                                             vmem_limit_bytes=VMEM_LIMIT),
        name="rwkv7",
    )(rkv, lora, h0, *params)


def _pad_rows(w, start, total):
    return jnp.pad(w, ((start, total - start - w.shape[0]), (0, 0)))


def kernel(x, meta_tokens, ffn1_norm, ffn1_w_gate, ffn1_w_up, ffn1_w_down, mix_norm, w_in, rwkv_mu, rwkv_w0, rwkv_w_up, rwkv_a0, rwkv_a_up, rwkv_g_up, rwkv_k_k, rwkv_k_a, rwkv_r_k, rwkv_lnx_w, rwkv_lnx_b, w_out, ffn2_norm, ffn2_w_gate, ffn2_w_up, ffn2_w_down, final_norm):
    b, l, _ = x.shape
    depth = ffn1_norm.shape[0]
    assert depth == 1, "stacked layers would need the meta rows carried through the mixers"
    row = lambda p: p.reshape(1, -1).astype(F32)
    n_rkv = 3 * RWKV_WIDTH
    n_lora = W_LORA + A_LORA + G_LORA

    h = x.reshape(b * l, D_MODEL)
    h_meta = _pad_rows(meta_tokens.astype(F32), META_ROWS - N_META, META_ROWS)
    for d in range(depth):
        w_qkv = w_in[d][:, :3 * SB_WIDTH]
        w_qkv = w_qkv.at[:, :SB_WIDTH].multiply(HEAD_DIM ** -0.5 * LOG2_E).astype(BF16)
        w_rkv = w_in[d][:, 3 * SB_WIDTH:3 * SB_WIDTH + n_rkv].astype(BF16)
        w_lora = jnp.pad(w_in[d][:, 3 * SB_WIDTH + n_rkv:],
                         ((0, 0), (0, LORA_PAD - n_lora))).astype(BF16)
        mu_rkv = row(rwkv_mu[d][:n_rkv])
        mu_lora = jnp.pad(row(rwkv_mu[d][n_rkv:]), ((0, 0), (0, LORA_PAD - n_lora)))
        rwkv_params = (
            row(rwkv_w0[d]), _pad_rows(rwkv_w_up[d], 0, LORA_PAD),
            row(rwkv_a0[d]), _pad_rows(rwkv_a_up[d], W_LORA, LORA_PAD),
            _pad_rows(rwkv_g_up[d], W_LORA + A_LORA, LORA_PAD), row(rwkv_k_k[d]),
            row(rwkv_k_a[d]), row(rwkv_r_k[d]), row(rwkv_lnx_w[d]), row(rwkv_lnx_b[d]))
        ffn1_w = (row(ffn1_norm[d]), ffn1_w_gate[d].astype(BF16), ffn1_w_up[d].astype(BF16),
                  ffn1_w_down[d].astype(BF16))
        ffn2_w = (row(ffn2_norm[d]), ffn2_w_gate[d].astype(BF16), ffn2_w_up[d].astype(BF16),
                  ffn2_w_down[d].astype(BF16))
        wo_sb = w_out[d][:SB_WIDTH].astype(BF16)
        wo_rw = w_out[d][SB_WIDTH:].astype(BF16)
        gf = row(final_norm)

        h = _ffn1(h, *ffn1_w)
        h_meta = _ffn1(h_meta, *ffn1_w)
        in_w = (row(mix_norm[d]), w_qkv, w_rkv, w_lora, mu_rkv, mu_lora)
        qkv_m, rkv_m, lora_m, last_rkv_m, last_lora_m = _in_proj(
            h_meta, META_ROWS, *in_w, jnp.zeros((1, n_rkv), F32), jnp.zeros((1, LORA_PAD), F32))
        qkv, rkv, lora, _, _ = _in_proj(h, l, *in_w, last_rkv_m, last_lora_m)

        sb = _sb_attention(qkv.reshape(b, l, -1), qkv_m)

        zero_state = jnp.zeros((N_PAIRS, PAIR, PAIR), F32)
        _, h_meta_state = _rwkv(rkv_m[None], lora_m[None], zero_state, rwkv_params)
        rw, _ = _rwkv(rkv.reshape(b, l, -1), lora.reshape(b, l, -1), h_meta_state[0],
                      rwkv_params)

        h = _ffn2(h, sb.reshape(b * l, -1), rw.reshape(b * l, -1), wo_sb, wo_rw, *ffn2_w, gf)
    return h.reshape(b, l, D_MODEL)
```

```python
import functools

import jax
import jax.numpy as jnp
from jax import lax
from jax.experimental import pallas as pl
from jax.experimental.pallas import tpu as pltpu

D_MODEL = 1024
N_META = 16
HEAD_DIM = 64
SB_WIDTH = 512
RWKV_WIDTH = 512
D_FF = 2816
W_LORA = 32
A_LORA = 32
G_LORA = 96
RMS_EPS = 1e-6
LNX_EPS = 64e-5

LANES = 128
META_ROWS = 128
LORA_PAD = 256
CHUNK = 64
PAIR = 2 * HEAD_DIM
N_PAIRS = RWKV_WIDTH // PAIR
VMEM_LIMIT = 56 * 1024 * 1024
TOKEN_TILE = 512
RWKV_TILE = 512
SB_Q_TILE = 4096
SB_K_TILE = 256
SB_GROUP = 2

F32 = jnp.float32
BF16 = jnp.bfloat16
LOG2_E = 1.4426950408889634
EXP2_CLAMP = 126.0
DEAD_LOG2 = -150.0


def _const_spec(shape):
    zeros = (0,) * len(shape)
    return pl.BlockSpec(shape, lambda *_: zeros, pipeline_mode=pl.Buffered(1))


def _rms(x, g):
    ms = jnp.mean(x * x, axis=-1, keepdims=True)
    return x * lax.rsqrt(ms + RMS_EPS) * g


def _dot(a, b):
    return jnp.dot(a, b, preferred_element_type=F32)


def _bdot(a, b):
    return _dot(a.astype(BF16), b.astype(BF16))


def _dot_nt(a, b):
    return lax.dot_general(a, b, (((1,), (1,)), ((), ())), preferred_element_type=F32)


def _dot_tn(a, b):
    return lax.dot_general(a, b, (((0,), (0,)), ((), ())), preferred_element_type=F32)


def _swiglu_half_step(h, g, wg_ref, wu_ref, wd_ref):
    n = _rms(h, g).astype(BF16)
    gate = _dot(n, wg_ref[...])
    up = _dot(n, wu_ref[...])
    act = (gate * jax.nn.sigmoid(gate) * up).astype(BF16)
    return h + 0.5 * _dot(act, wd_ref[...])


def _ffn1_kernel(x_ref, g_ref, wg_ref, wu_ref, wd_ref, o_ref):
    o_ref[...] = _swiglu_half_step(x_ref[...], g_ref[...], wg_ref, wu_ref, wd_ref)


def _ffn2_kernel(h_ref, sb_ref, rw_ref, wo_sb_ref, wo_rw_ref, g_ref, wg_ref, wu_ref, wd_ref,
                 gf_ref, o_ref):
    h = h_ref[...] + _dot(sb_ref[...], wo_sb_ref[...]) + _dot(rw_ref[...], wo_rw_ref[...])
    h = _swiglu_half_step(h, g_ref[...], wg_ref, wu_ref, wd_ref)
    o_ref[...] = _rms(h, gf_ref[...])


def _in_proj_kernel(h_ref, g_ref, wqkv_ref, wrkv_ref, wlora_ref, mu_rkv_ref, mu_lora_ref,
                    first_rkv_ref, first_lora_ref, qkv_ref, rkv_ref, lora_ref, last_rkv_ref,
                    last_lora_ref, prev_rkv_sc, prev_lora_sc, *, tiles_per_seq):
    tile = h_ref.shape[0]

    @pl.when(pl.program_id(0) % tiles_per_seq == 0)
    def _():
        prev_rkv_sc[...] = first_rkv_ref[...]
        prev_lora_sc[...] = first_lora_ref[...]

    n = _rms(h_ref[...], g_ref[...]).astype(BF16)
    qkv_ref[...] = _dot(n, wqkv_ref[...]).astype(BF16)
    first_row = lax.broadcasted_iota(jnp.int32, (tile, 1), 0) == 0

    def shifted(w_ref, mu_ref, prev_sc, out_ref, last_ref):
        x = _dot(n, w_ref[...])
        x_prev = jnp.where(first_row, prev_sc[...], pltpu.roll(x, 1, axis=0))
        out_ref[...] = x + (x_prev - x) * mu_ref[...]
        prev_sc[...] = x[tile - 1:tile, :]
        last_ref[...] = x[tile - 1:tile, :]

    shifted(wrkv_ref, mu_rkv_ref, prev_rkv_sc, rkv_ref, last_rkv_ref)
    shifted(wlora_ref, mu_lora_ref, prev_lora_sc, lora_ref, last_lora_ref)


def _token_tile(n_tokens):
    t = min(TOKEN_TILE, n_tokens)
    assert n_tokens % t == 0 and t % 8 == 0, (n_tokens, t)
    return t


def _row_spec(t, width):
    return pl.BlockSpec((t, width), lambda i: (i, 0))


def _token_params():
    return pltpu.CompilerParams(dimension_semantics=("parallel",), vmem_limit_bytes=VMEM_LIMIT)


def _ffn1(x2d, g, wg, wu, wd):
    n = x2d.shape[0]
    t = _token_tile(n)
    return pl.pallas_call(
        _ffn1_kernel,
        grid=(n // t,),
        in_specs=[_row_spec(t, D_MODEL), _const_spec((1, D_MODEL)), _const_spec(wg.shape),
                  _const_spec(wu.shape), _const_spec(wd.shape)],
        out_specs=_row_spec(t, D_MODEL),
        out_shape=jax.ShapeDtypeStruct((n, D_MODEL), F32),
        compiler_params=_token_params(),
        name="ffn1",
    )(x2d, g, wg, wu, wd)


def _ffn2(h2d, sb2d, rw2d, wo_sb, wo_rw, g, wg, wu, wd, gf):
    n = h2d.shape[0]
    t = _token_tile(n)
    return pl.pallas_call(
        _ffn2_kernel,
        grid=(n // t,),
        in_specs=[_row_spec(t, D_MODEL), _row_spec(t, SB_WIDTH), _row_spec(t, RWKV_WIDTH),
                  _const_spec(wo_sb.shape), _const_spec(wo_rw.shape), _const_spec((1, D_MODEL)),
                  _const_spec(wg.shape), _const_spec(wu.shape), _const_spec(wd.shape),
                  _const_spec((1, D_MODEL))],
        out_specs=_row_spec(t, D_MODEL),
        out_shape=jax.ShapeDtypeStruct((n, D_MODEL), F32),
        compiler_params=_token_params(),
        name="ffn2",
    )(h2d, sb2d, rw2d, wo_sb, wo_rw, g, wg, wu, wd, gf)


def _in_proj(h2d, seq_len, g, wqkv, wrkv, wlora, mu_rkv, mu_lora, first_rkv, first_lora):
    n = h2d.shape[0]
    t = _token_tile(seq_len)
    n_rkv = 3 * RWKV_WIDTH
    last = lambda w: pl.BlockSpec((1, w), lambda i: (0, 0))
    return pl.pallas_call(
        functools.partial(_in_proj_kernel, tiles_per_seq=seq_len // t),
        grid=(n // t,),
        in_specs=[_row_spec(t, D_MODEL), _const_spec((1, D_MODEL)), _const_spec(wqkv.shape),
                  _const_spec(wrkv.shape), _const_spec(wlora.shape), _const_spec(mu_rkv.shape),
                  _const_spec(mu_lora.shape), _const_spec(first_rkv.shape),
                  _const_spec(first_lora.shape)],
        out_specs=[_row_spec(t, 3 * SB_WIDTH), _row_spec(t, n_rkv), _row_spec(t, LORA_PAD),
                   last(n_rkv), last(LORA_PAD)],
        out_shape=[jax.ShapeDtypeStruct((n, 3 * SB_WIDTH), BF16),
                   jax.ShapeDtypeStruct((n, n_rkv), F32),
                   jax.ShapeDtypeStruct((n, LORA_PAD), F32),
                   jax.ShapeDtypeStruct((1, n_rkv), F32),
                   jax.ShapeDtypeStruct((1, LORA_PAD), F32)],
        scratch_shapes=[pltpu.VMEM((1, n_rkv), F32), pltpu.VMEM((1, LORA_PAD), F32)],
        compiler_params=pltpu.CompilerParams(dimension_semantics=("arbitrary",),
                                             vmem_limit_bytes=VMEM_LIMIT),
        name="in_proj",
    )(h2d, g, wqkv, wrkv, wlora, mu_rkv, mu_lora, first_rkv, first_lora)


def _sb_attn_kernel(q_ref, k_ref, v_ref, km_ref, vm_ref, o_ref, qh_sc, acc_sc, rest_sc, *, tq, tk,
                    single_tile):
    i = 0 if single_tile else pl.program_id(2)
    q = q_ref[0]
    low = lax.broadcasted_iota(jnp.int32, (1, LANES), 1) < HEAD_DIM
    zero = jnp.zeros_like(q)
    qh_sc[0] = jnp.where(low, q, zero)
    qh_sc[1] = jnp.where(low, zero, q)
    acc_sc[...] = jnp.zeros_like(acc_sc)
    rest_sc[...] = jnp.zeros_like(rest_sc)

    row = lax.broadcasted_iota(jnp.int32, (tk, tk), 0)
    col = lax.broadcasted_iota(jnp.int32, (tk, tk), 1)
    suffix_w = jnp.where(row > col, -1.0, 0.0).astype(BF16)
    diag_mask = col < row
    meta_mask = lax.broadcasted_iota(jnp.int32, (1, META_ROWS), 1) >= META_ROWS - N_META

    def sweep(chains, w, mask):
        if mask is not None:
            mask = jnp.concatenate([mask, mask], axis=0) if mask.shape[0] > 1 else mask
        z = [_dot_nt(jnp.concatenate([qh_sc[0, rows, :], qh_sc[1, rows, :]], axis=0), kb)
             for rows, kb, _, _ in chains]
        sp = [jnp.maximum(x, jnp.log2(1.0 + jnp.exp2(jnp.minimum(x, EXP2_CLAMP)))) for x in z]
        if mask is not None:
            sp = [jnp.where(mask, x, 0.0) for x in sp]
        sizes = [x.shape[0] for x in sp]
        cum = _dot(jnp.concatenate([x.astype(BF16) for x in sp], axis=0), w)
        cum = [cum[sum(sizes[:n]):sum(sizes[:n + 1])] for n in range(len(sp))]
        p = [jnp.exp2((a - b + c).astype(BF16)) for a, b, c in zip(z, sp, cum)]
        if mask is not None:
            p = [jnp.where(mask, x, jnp.zeros_like(x)) for x in p]
        pv = [_dot(x, c[2]) for x, c in zip(p, chains)]
        for (rows, _, _, valid), s, c, o in zip(chains, sp, cum, pv):
            n = o.shape[0] // 2
            for hh in range(2):
                half = slice(hh * n, (hh + 1) * n)
                rest = rest_sc[hh, rows, :]
                scale = jnp.exp2(rest)
                total = c[half, :1] - s[half, :1]
                if valid is not None:
                    scale = scale * valid
                    total = total * valid
                acc_sc[hh, rows, :] += scale * o[half]
                rest_sc[hh, rows, :] = rest + total

    n_blk = tq // tk

    def wave(s, mask):
        for rb0 in range(0, n_blk, SB_GROUP):
            chains = []
            for rb in range(rb0, rb0 + SB_GROUP):
                j = i * n_blk + rb - s
                if isinstance(j, int):
                    if j < 0:
                        continue
                    valid, start = None, j * tk
                else:
                    valid = None if mask is not None else (j >= 0).astype(F32)
                    start = pl.multiple_of(jnp.maximum(j, 0) * tk, tk)
                kb = k_ref[0, pl.ds(start, tk), :]
                vb = v_ref[0, pl.ds(start, tk), :]
                chains.append((slice(rb * tk, (rb + 1) * tk), kb, vb, valid))
            sweep(chains, suffix_w, mask)

    def stick_left(rb):
        rows = slice(rb * tk, (rb + 1) * tk)
        return jnp.maximum(jnp.max(rest_sc[0, rows, :]), jnp.max(rest_sc[1, rows, :]))

    def alive(s, left=None):
        left = [jnp.where(i * n_blk + rb - s >= 0,
                          stick_left(rb) if left is None else left[rb], -jnp.inf)
                for rb in range(n_blk)]
        return functools.reduce(jnp.maximum, left) > DEAD_LOG2

    wave(0, diag_mask)
    wave(1, None)
    left = [stick_left(rb) for rb in range(n_blk)]
    lax.while_loop(lambda c: c[1],
                   lambda c: (wave(c[0], None), (c[0] + 1, alive(c[0] + 1)))[1],
                   (jnp.int32(2), alive(2, left)))

    for rb in range(n_blk):
        @pl.when(left[rb] > DEAD_LOG2)
        def _():
            sweep([(slice(rb * tk, (rb + 1) * tk), km_ref[...], vm_ref[...], None)],
                  suffix_w[tk - META_ROWS:, tk - META_ROWS:], meta_mask)

    o_ref[0] = jnp.where(low, acc_sc[0], acc_sc[1]).astype(o_ref.dtype)


def _sb_attention(qkv, qkv_meta):
    b, l, _ = qkv.shape
    tq = min(SB_Q_TILE, l)
    tk = min(SB_K_TILE, l)
    assert l % tq == 0 and tq % (tk * SB_GROUP) == 0 and tk >= META_ROWS
    n_pairs = SB_WIDTH // LANES
    return pl.pallas_call(
        functools.partial(_sb_attn_kernel, tq=tq, tk=tk, single_tile=l == tq),
        grid=(b, n_pairs, l // tq),
        in_specs=[
            pl.BlockSpec((1, tq, LANES), lambda bi, hp, i: (bi, i, hp)),
            pl.BlockSpec((1, l, LANES), lambda bi, hp, i: (bi, 0, n_pairs + hp)),
            pl.BlockSpec((1, l, LANES), lambda bi, hp, i: (bi, 0, 2 * n_pairs + hp)),
            pl.BlockSpec((META_ROWS, LANES), lambda bi, hp, i: (0, n_pairs + hp)),
            pl.BlockSpec((META_ROWS, LANES), lambda bi, hp, i: (0, 2 * n_pairs + hp)),
        ],
        out_specs=pl.BlockSpec((1, tq, LANES), lambda bi, hp, i: (bi, i, hp)),
        out_shape=jax.ShapeDtypeStruct((b, l, SB_WIDTH), BF16),
        scratch_shapes=[pltpu.VMEM((2, tq, LANES), BF16), pltpu.VMEM((2, tq, LANES), F32),
                        pltpu.VMEM((2, tq, 1), F32)],
        compiler_params=pltpu.CompilerParams(
            dimension_semantics=("parallel", "parallel", "arbitrary"),
            vmem_limit_bytes=VMEM_LIMIT),
        name="sb_attention",
    )(qkv, qkv, qkv, qkv_meta, qkv_meta)


def _rwkv_kernel(rkv_ref, lora_ref, h0_ref, w0_ref, wup_ref, a0_ref, aup_ref, gup_ref, kk_ref,
                 ka_ref, rk_ref, lnw_ref, lnb_ref, y_ref, hT_ref, h_sc, r_sc, k_sc, v_sc, kk_sc,
                 a_sc, lw_sc, y_sc, *, tile):
    c_w = RWKV_WIDTH

    @pl.when(pl.program_id(1) == 0)
    def _():
        h_sc[...] = h0_ref[...]

    rr = lax.broadcasted_iota(jnp.int32, (PAIR, PAIR), 0)
    cc = lax.broadcasted_iota(jnp.int32, (PAIR, PAIR), 1)
    same_head = (rr < HEAD_DIM) == (cc < HEAD_DIM)
    pair_sum = jnp.where(same_head, 1.0, 0.0).astype(BF16)

    def head_sums(x):
        x = x.astype(BF16)
        return jnp.concatenate([_dot(x[:, PAIR * pr:PAIR * (pr + 1)], pair_sum)
                                for pr in range(N_PAIRS)], axis=1)

    def prepare(r0, n):
        rows = slice(r0, r0 + n)
        p = rkv_ref[0, rows, :]
        lora = lora_ref[0, rows, :]
        r = p[:, :c_w]
        k = p[:, c_w:2 * c_w]
        w = -jax.nn.softplus(-(w0_ref[...] + _bdot(jnp.tanh(lora), wup_ref[...]))) - 0.5
        a = jax.nn.sigmoid(a0_ref[...] + _bdot(lora, aup_ref[...]))
        kk = k * kk_ref[...]
        r_sc[rows, :] = r
        k_sc[rows, :] = k * (1.0 + (a - 1.0) * ka_ref[...])
        v_sc[rows, :] = p[:, 2 * c_w:]
        kk_sc[rows, :] = kk * lax.rsqrt(jnp.maximum(head_sums(kk * kk), 1e-24))
        a_sc[rows, :] = a
        lw_sc[rows, :] = -jnp.exp(w)

    strict_lower = same_head & (cc < rr)
    lower = same_head & (cc <= rr)
    eye_mask = rr == cc
    eye = jnp.where(eye_mask, 1.0, 0.0)
    tr = lax.broadcasted_iota(jnp.int32, (CHUNK, CHUNK), 0)
    tc = lax.broadcasted_iota(jnp.int32, (CHUNK, CHUNK), 1)
    cumsum_w = jnp.where(tc <= tr, 1.0, 0.0).astype(BF16)
    zero = jnp.zeros((PAIR, PAIR), BF16)

    def chunk_operands(c):
        rows = slice(c * CHUNK, (c + 1) * CHUNK)
        lw = lw_sc[rows, :]
        lw_hi = lw.astype(BF16)
        g = _dot(cumsum_w, lw_hi) + _dot(cumsum_w, (lw - lw_hi.astype(F32)).astype(BF16))
        g_end = g[CHUNK - 1:CHUNK, :]
        decay_out = jnp.exp(-g)
        decay_end = jnp.exp(g_end - g)
        kk_c = kk_sc[rows, :]
        kka = kk_c * a_sc[rows, :]
        k_c = k_sc[rows, :]
        full = dict(
            a_bar=(-kk_c * jnp.exp(g - lw)).astype(BF16),
            r_bar=(r_sc[rows, :] * jnp.exp(g)).astype(BF16),
            b_til=(kka * decay_out).astype(BF16), k_til=(k_c * decay_out).astype(BF16),
            b_hat=(kka * decay_end).astype(BF16), k_hat=(k_c * decay_end).astype(BF16),
            v=v_sc[rows, :].astype(BF16))
        gamma_end = jnp.exp(g_end)
        masked = ("a_bar", "r_bar", "v", "b_hat", "k_hat")
        out = []
        for pr in range(N_PAIRS):
            sl = slice(PAIR * pr, PAIR * (pr + 1))
            ops = {}
            for name, x in full.items():
                x2 = jnp.concatenate([x[:, sl], x[:, sl]], axis=0)
                ops[name] = jnp.where(same_head, x2, zero) if name in masked else x2
            ops.update(rows=rows, sl=sl, pr=pr, gamma_end=gamma_end[:, sl],
                       r_pair=full["r_bar"][:, sl])
            out.append(ops)
        return out

    def state_free(chunks, out):
        chains = [ops for c in chunks for ops in chunk_operands(c)]
        s1 = [_dot_nt(jnp.concatenate([o["a_bar"], o["r_bar"]], axis=0),
                      jnp.concatenate([o["b_til"], o["k_til"]], axis=0)) for o in chains]
        n_mat = [jnp.where(strict_lower, s[:PAIR, :PAIR], 0.0) for s in s1]
        a_ak = [jnp.where(strict_lower, s[:PAIR, PAIR:], 0.0).astype(BF16) for s in s1]
        a_rb = [jnp.where(lower, s[PAIR:, :PAIR], 0.0).astype(BF16) for s in s1]
        a_rk = [jnp.where(lower, s[PAIR:, PAIR:], 0.0).astype(BF16) for s in s1]
        yield
        inv = [eye + n for n in n_mat]
        power = [n.astype(BF16) for n in n_mat]
        for _ in range(5):
            power = [_dot(p, p).astype(BF16) for p in power]
            yield
            inv = [i + _dot(i.astype(BF16), p) for i, p in zip(inv, power)]
            yield
        akv = [_dot(a, o["v"]).astype(BF16) for a, o in zip(a_ak, chains)]
        yield
        pq = [_dot(i.astype(BF16), jnp.concatenate([o["a_bar"], x], axis=1)).astype(BF16)
              for i, o, x in zip(inv, chains, akv)]
        yield
        pqv = [jnp.concatenate([x, jnp.concatenate([zero, o["v"]], axis=1)], axis=0)
               for x, o in zip(pq, chains)]
        ef = [_dot(jnp.concatenate([b, k], axis=1), x) for b, k, x in zip(a_rb, a_rk, pqv)]
        ef = [x[:CHUNK] + x[CHUNK:] for x in ef]
        yield
        mg = [_dot_tn(jnp.concatenate([o["b_hat"], o["k_hat"]], axis=0), x)
              for o, x in zip(chains, pqv)]
        out.extend(zip(chains, ef, mg))

    def advance_state(items):
        for o, x, m in items:
            e_mat = x[:, :PAIR] + o["r_pair"].astype(F32)
            m_mat = m[:, :PAIR] + jnp.where(eye_mask, o["gamma_end"], 0.0)
            eh = _bdot(jnp.concatenate([e_mat, m_mat], axis=0), h_sc[o["pr"]])
            y_sc[o["rows"], o["sl"]] = eh[:CHUNK] + x[:, PAIR:]
            h_sc[o["pr"]] = eh[CHUNK:] + m[:, PAIR:]

    def finish(r0, n):
        rows = slice(r0, r0 + n)
        gate = _bdot(jax.nn.sigmoid(lora_ref[0, rows, :]), gup_ref[...])
        y = y_sc[rows, :]
        inv_n = 1.0 / HEAD_DIM
        d = y - head_sums(y) * inv_n
        yn = d * lax.rsqrt(head_sums(d * d) * inv_n + LNX_EPS)
        yn = yn * lnw_ref[...] + lnb_ref[...]
        bonus = head_sums(r_sc[rows, :] * k_sc[rows, :] * rk_ref[...]) * v_sc[rows, :]
        y_ref[0, rows, :] = ((yn + bonus) * gate).astype(y_ref.dtype)

    n_chunks = tile // CHUNK
    first = list(range((n_chunks + 1) // 2))
    second = list(range(len(first), n_chunks))
    n_first = len(first) * CHUNK
    prepare(0, n_first)
    done_a, done_b = [], []
    stages_a = state_free(first, done_a)
    next(stages_a)
    if second:
        prepare(n_first, tile - n_first)
    for _ in stages_a:
        pass
    pending = [functools.partial(advance_state, done_a[c * N_PAIRS:(c + 1) * N_PAIRS])
               for c in range(len(first))] + [functools.partial(finish, 0, n_first)]
    for _ in state_free(second, done_b) if second else ():
        if pending:
            pending.pop(0)()
    while pending:
        pending.pop(0)()
    for c in range(len(second)):
        advance_state(done_b[c * N_PAIRS:(c + 1) * N_PAIRS])
    if second:
        finish(n_first, tile - n_first)
    hT_ref[0] = h_sc[...]


def _rwkv(rkv, lora, h0, params):
    b, l, _ = rkv.shape
    tile = min(RWKV_TILE, l)
    assert l % tile == 0 and tile % CHUNK == 0
    seq = lambda w: pl.BlockSpec((1, tile, w), lambda bi, t: (bi, t, 0))
    scr = lambda: pltpu.VMEM((tile, RWKV_WIDTH), F32)
    return pl.pallas_call(
        functools.partial(_rwkv_kernel, tile=tile),
        grid=(b, l // tile),
        in_specs=[seq(3 * RWKV_WIDTH), seq(LORA_PAD), _const_spec(h0.shape)]
                 + [_const_spec(p.shape) for p in params],
        out_specs=[seq(RWKV_WIDTH),
                   pl.BlockSpec((1, N_PAIRS, PAIR, PAIR), lambda bi, t: (bi, 0, 0, 0))],
        out_shape=[jax.ShapeDtypeStruct((b, l, RWKV_WIDTH), BF16),
                   jax.ShapeDtypeStruct((b, N_PAIRS, PAIR, PAIR), F32)],
        scratch_shapes=[pltpu.VMEM((N_PAIRS, PAIR, PAIR), F32),
                        scr(), scr(), scr(), scr(), scr(), scr(), scr()],
        compiler_params=pltpu.CompilerParams(dimension_semantics=("parallel", "arbitrary"),
                                             vmem_limit_bytes=VMEM_LIMIT),
        name="rwkv7",
    )(rkv, lora, h0, *params)


def _pad_rows(w, start, total):
    return jnp.pad(w, ((start, total - start - w.shape[0]), (0, 0)))


def kernel(x, meta_tokens, ffn1_norm, ffn1_w_gate, ffn1_w_up, ffn1_w_down, mix_norm, w_in, rwkv_mu, rwkv_w0, rwkv_w_up, rwkv_a0, rwkv_a_up, rwkv_g_up, rwkv_k_k, rwkv_k_a, rwkv_r_k, rwkv_lnx_w, rwkv_lnx_b, w_out, ffn2_norm, ffn2_w_gate, ffn2_w_up, ffn2_w_down, final_norm):
    b, l, _ = x.shape
    depth = ffn1_norm.shape[0]
    assert depth == 1, "stacked layers would need the meta rows carried through the mixers"
    row = lambda p: p.reshape(1, -1).astype(F32)
    n_rkv = 3 * RWKV_WIDTH
    n_lora = W_LORA + A_LORA + G_LORA

    h = x.reshape(b * l, D_MODEL)
    h_meta = _pad_rows(meta_tokens.astype(F32), META_ROWS - N_META, META_ROWS)
    for d in range(depth):
        w_qkv = w_in[d][:, :3 * SB_WIDTH]
        w_qkv = w_qkv.at[:, :SB_WIDTH].multiply(HEAD_DIM ** -0.5 * LOG2_E).astype(BF16)
        w_rkv = w_in[d][:, 3 * SB_WIDTH:3 * SB_WIDTH + n_rkv].astype(BF16)
        w_lora = jnp.pad(w_in[d][:, 3 * SB_WIDTH + n_rkv:],
                         ((0, 0), (0, LORA_PAD - n_lora))).astype(BF16)
        mu_rkv = row(rwkv_mu[d][:n_rkv])
        mu_lora = jnp.pad(row(rwkv_mu[d][n_rkv:]), ((0, 0), (0, LORA_PAD - n_lora)))
        rwkv_params = (
            row(rwkv_w0[d]), _pad_rows(rwkv_w_up[d], 0, LORA_PAD),
            row(rwkv_a0[d]), _pad_rows(rwkv_a_up[d], W_LORA, LORA_PAD),
            _pad_rows(rwkv_g_up[d], W_LORA + A_LORA, LORA_PAD), row(rwkv_k_k[d]),
            row(rwkv_k_a[d]), row(rwkv_r_k[d]), row(rwkv_lnx_w[d]), row(rwkv_lnx_b[d]))
        ffn1_w = (row(ffn1_norm[d]), ffn1_w_gate[d].astype(BF16), ffn1_w_up[d].astype(BF16),
                  ffn1_w_down[d].astype(BF16))
        ffn2_w = (row(ffn2_norm[d]), ffn2_w_gate[d].astype(BF16), ffn2_w_up[d].astype(BF16),
                  ffn2_w_down[d].astype(BF16))
        wo_sb = w_out[d][:SB_WIDTH].astype(BF16)
        wo_rw = w_out[d][SB_WIDTH:].astype(BF16)
        gf = row(final_norm)

        h = _ffn1(h, *ffn1_w)
        h_meta = _ffn1(h_meta, *ffn1_w)
        in_w = (row(mix_norm[d]), w_qkv, w_rkv, w_lora, mu_rkv, mu_lora)
        qkv_m, rkv_m, lora_m, last_rkv_m, last_lora_m = _in_proj(
            h_meta, META_ROWS, *in_w, jnp.zeros((1, n_rkv), F32), jnp.zeros((1, LORA_PAD), F32))
        qkv, rkv, lora, _, _ = _in_proj(h, l, *in_w, last_rkv_m, last_lora_m)

        sb = _sb_attention(qkv.reshape(b, l, -1), qkv_m)

        zero_state = jnp.zeros((N_PAIRS, PAIR, PAIR), F32)
        _, h_meta_state = _rwkv(rkv_m[None], lora_m[None], zero_state, rwkv_params)
        rw, _ = _rwkv(rkv.reshape(b, l, -1), lora.reshape(b, l, -1), h_meta_state[0],
                      rwkv_params)

        h = _ffn2(h, sb.reshape(b * l, -1), rw.reshape(b * l, -1), wo_sb, wo_rw, *ffn2_w, gf)
    return h.reshape(b, l, D_MODEL)
```
